```python
import math
import jax, jax.numpy as jnp
from jax import lax
import numpy as np

D_MODEL = 2048
BATCH = 2
SEQ = 8192
DEPTH = 4

HEAD_DIM = 128
N_HEADS_DIFF = 6
N_HEADS_MOBA = 4
N_HEADS_DIL = 6
N_SELF_HEADS = N_HEADS_DIFF + N_HEADS_MOBA + N_HEADS_DIL
W_DIFF = N_HEADS_DIFF * HEAD_DIM
W_MOBA = N_HEADS_MOBA * HEAD_DIM
W_DIL = N_HEADS_DIL * HEAD_DIM
MIX_WIDTH = W_DIFF + W_MOBA + W_DIL
DIFF_QK_DIM = HEAD_DIM // 2
Q_BLOCK = 128
MOBA_BLOCK = 256
MOBA_TOPK = 3
MOBA_Q_CHUNK = 64
DIL_PAIRS = ((128, 1), (512, 4), (2048, 16))
DIL_BLOCK = 128
N_BUCKETS = 32
REL_MAX_DIST = 2048
N_MEM = 256
N_MEM_HEADS = 4
MEM_WIDTH = N_MEM_HEADS * HEAD_DIM
D_FF = 5632
CONV_WIDTH = 3
NORM_EPS = 1e-6
NEG_INF = -1e30

kernel_name = 'hybrid_diff_moba_dilated_trunk'


def rms_norm(x, g):
    xf = x.astype(jnp.float32)
    y = xf * lax.rsqrt(jnp.mean(xf * xf, axis=-1, keepdims=True) + NORM_EPS)
    return (y * g.astype(jnp.float32)).astype(x.dtype)


def t5_bucket(dist):
    n = jnp.maximum(dist, 0)
    max_exact = N_BUCKETS // 2
    nf = jnp.maximum(n, 1).astype(jnp.float32)
    log_ratio = jnp.log(nf / max_exact) / math.log(REL_MAX_DIST / max_exact)
    large = max_exact + (log_ratio * (N_BUCKETS - max_exact)).astype(jnp.int32)
    large = jnp.minimum(large, N_BUCKETS - 1)
    return jnp.where(n < max_exact, n, large)


def rel_bias(table, dist):
    return jnp.moveaxis(table.astype(jnp.float32)[t5_bucket(dist)], -1, 0)


def diff_attention(q, k, v, lam, sub_g, lam_init, table):
    B, S, H, _, dqk = q.shape
    nq = S // Q_BLOCK
    scale = dqk ** -0.5
    qb = q.reshape(B, nq, Q_BLOCK, H, 2, dqk).transpose(1, 0, 3, 4, 2, 5)
    kt = k.transpose(0, 2, 3, 1, 4)
    vt = v.transpose(0, 2, 1, 3)
    kpos = jnp.arange(S)

    def block(args):
        i, qi = args
        qpos = i * Q_BLOCK + jnp.arange(Q_BLOCK)
        dist = qpos[:, None] - kpos[None, :]
        s = jnp.einsum('bhcqd,bhckd->bhcqk', qi, kt).astype(jnp.float32) * scale
        s = s + rel_bias(table, dist)[None, :, None]
        s = jnp.where(dist >= 0, s, NEG_INF)
        p = jax.nn.softmax(s, axis=-1)
        a = p[:, :, 0] - lam * p[:, :, 1]
        return jnp.einsum('bhqk,bhkd->bhqd', a.astype(vt.dtype), vt)

    o = lax.map(block, (jnp.arange(nq), qb))
    o = o.transpose(1, 0, 3, 2, 4).reshape(B, S, H, -1)
    return rms_norm(o, sub_g) * (1.0 - lam_init)


def moba_attention(q, k, v, table):
    B, S, H, D = q.shape
    Sp = -(-S // MOBA_BLOCK) * MOBA_BLOCK
    nb = Sp // MOBA_BLOCK
    ksel = min(MOBA_TOPK, nb)
    C = MOBA_Q_CHUNK
    nc = Sp // C
    pad = ((0, 0), (0, Sp - S), (0, 0), (0, 0))
    qt = jnp.pad(q, pad).transpose(0, 2, 1, 3)
    kt = jnp.pad(k, pad).transpose(0, 2, 1, 3)
    vt = jnp.pad(v, pad).transpose(0, 2, 1, 3)
    kb = kt.reshape(B, H, nb, MOBA_BLOCK, D)
    vb = vt.reshape(B, H, nb, MOBA_BLOCK, D)
    kmean = jnp.mean(kb.astype(jnp.float32), axis=3)
    gate = jnp.einsum('bhsd,bhnd->bhsn', qt.astype(jnp.float32), kmean)
    qblk = jnp.arange(Sp) // MOBA_BLOCK
    past = jnp.arange(nb)[None, :] < qblk[:, None]
    gate = jnp.where(past, gate, NEG_INF)
    _, sel = lax.top_k(gate, ksel)
    sel_ok = sel < qblk[:, None]
    qc = qt.reshape(B, H, nc, C, D).transpose(2, 0, 1, 3, 4)
    selc = sel.reshape(B, H, nc, C, ksel).transpose(2, 0, 1, 3, 4)
    okc = sel_ok.reshape(B, H, nc, C, ksel).transpose(2, 0, 1, 3, 4)
    scale = D ** -0.5
    tab_t = table.astype(jnp.float32).T
    bidx = jnp.arange(B)[:, None, None]
    hidx = jnp.arange(H)[None, :, None]
    koff = jnp.arange(MOBA_BLOCK)

    def chunk(args):
        i, qi, si, oki = args
        qpos = i * C + jnp.arange(C)
        own = (i * C) // MOBA_BLOCK
        flat = si.reshape(B, H, C * ksel)
        kg = kb[bidx, hidx, flat].reshape(B, H, C, ksel, MOBA_BLOCK, D)
        vg = vb[bidx, hidx, flat].reshape(B, H, C, ksel, MOBA_BLOCK, D)
        s_sel = jnp.einsum('bhqd,bhqjkd->bhqjk', qi, kg).astype(jnp.float32) * scale
        dist_sel = qpos[None, None, :, None, None] - (si[..., None] * MOBA_BLOCK + koff)
        s_sel = s_sel + tab_t[jnp.arange(H)[None, :, None, None, None], t5_bucket(dist_sel)]
        s_sel = jnp.where(oki[..., None], s_sel, NEG_INF)
        ko = lax.dynamic_slice_in_dim(kt, own * MOBA_BLOCK, MOBA_BLOCK, axis=2)
        vo = lax.dynamic_slice_in_dim(vt, own * MOBA_BLOCK, MOBA_BLOCK, axis=2)
        dist_own = qpos[:, None] - (own * MOBA_BLOCK + koff)[None, :]
        s_own = jnp.einsum('bhqd,bhkd->bhqk', qi, ko).astype(jnp.float32) * scale
        s_own = s_own + rel_bias(table, dist_own)[None]
        s_own = jnp.where(dist_own >= 0, s_own, NEG_INF)
        s = jnp.concatenate([s_sel.reshape(B, H, C, ksel * MOBA_BLOCK), s_own], axis=-1)
        p = jax.nn.softmax(s, axis=-1).astype(vt.dtype)
        p_sel = p[..., :ksel * MOBA_BLOCK].reshape(B, H, C, ksel, MOBA_BLOCK)
        p_own = p[..., ksel * MOBA_BLOCK:]
        return (jnp.einsum('bhqjk,bhqjkd->bhqd', p_sel, vg)
                + jnp.einsum('bhqk,bhkd->bhqd', p_own, vo))

    o = lax.map(chunk, (jnp.arange(nc), qc, selc, okc))
    o = o.transpose(1, 0, 3, 2, 4).reshape(B, Sp, H, D)
    return o[:, :S]


def dilated_branch(q, k, v, table, window, dil):
    B, H, S, D = q.shape
    L = S // dil
    steps = window // dil
    nb = -(-L // DIL_BLOCK)
    Lp = nb * DIL_BLOCK

    def sub(t):
        return t.reshape(B, H, L, dil, D).transpose(0, 1, 3, 2, 4)

    def windowed(t):
        tp = jnp.pad(sub(t), ((0, 0), (0, 0), (0, 0), (DIL_BLOCK, Lp - L), (0, 0)))
        tp = tp.reshape(B, H, dil, nb + 1, DIL_BLOCK, D)
        return jnp.concatenate([tp[:, :, :, :-1], tp[:, :, :, 1:]], axis=4)

    qs = jnp.pad(sub(q), ((0, 0), (0, 0), (0, 0), (0, Lp - L), (0, 0))).reshape(B, H, dil, nb, DIL_BLOCK, D)
    kw = windowed(k)
    vw = windowed(v)
    qi = jnp.arange(DIL_BLOCK)
    ki = jnp.arange(2 * DIL_BLOCK) - DIL_BLOCK
    j = qi[:, None] - ki[None, :]
    kidx = jnp.arange(nb)[:, None, None] * DIL_BLOCK + ki[None, None, :]
    valid = (j >= 0) & (j <= steps) & (kidx >= 0)
    s = jnp.einsum('bhrnqd,bhrnkd->bhrnqk', qs, kw).astype(jnp.float32) * D ** -0.5
    s = s + rel_bias(table, j * dil)[None, :, None, None]
    s = jnp.where(valid, s, NEG_INF)
    lse = jax.nn.logsumexp(s, axis=-1)
    p = jnp.exp(s - lse[..., None]).astype(v.dtype)
    o = jnp.einsum('bhrnqk,bhrnkd->bhrnqd', p, vw)
    o = o.reshape(B, H, dil, Lp, D)[:, :, :, :L].transpose(0, 1, 3, 2, 4).reshape(B, H, S, D)
    lse = lse.reshape(B, H, dil, Lp)[:, :, :, :L].transpose(0, 1, 3, 2).reshape(B, H, S)
    return o, lse


def dilated_attention(q, k, v, table):
    qt, kt, vt = (t.transpose(0, 2, 1, 3) for t in (q, k, v))
    outs, lses = [], []
    for window, dil in DIL_PAIRS:
        o, lse = dilated_branch(qt, kt, vt, table, window, dil)
        outs.append(o)
        lses.append(lse)
    wts = jax.nn.softmax(jnp.stack(lses), axis=0)
    o = jnp.sum(wts[..., None] * jnp.stack(outs).astype(jnp.float32), axis=0)
    return o.astype(q.dtype).transpose(0, 2, 1, 3)


def memory_cross_attention(h, mem_n, w_q, w_kv, w_o):
    B, S, _ = h.shape
    q = (h @ w_q).reshape(B, S, N_MEM_HEADS, HEAD_DIM)
    kv = (mem_n @ w_kv).reshape(B, -1, 2, N_MEM_HEADS, HEAD_DIM)
    k, v = kv[:, :, 0], kv[:, :, 1]
    s = jnp.einsum('bshd,bmhd->bhsm', q, k).astype(jnp.float32) * HEAD_DIM ** -0.5
    p = jax.nn.softmax(s, axis=-1).astype(v.dtype)
    o = jnp.einsum('bhsm,bmhd->bshd', p, v).reshape(B, S, MEM_WIDTH)
    return o @ w_o


def conv_glu_ffn(h, w_up, conv_w, conv_b, w_down):
    gu = h @ w_up
    g, u = jnp.split(gu, 2, axis=-1)
    g = lax.conv_general_dilated(
        g, conv_w[:, None, :].astype(g.dtype), window_strides=(1,),
        padding=[(CONV_WIDTH - 1, 0)], dimension_numbers=('NWC', 'WIO', 'NWC'),
        feature_group_count=D_FF) + conv_b
    return (jax.nn.silu(g) * u) @ w_down


def setup_inputs(seed: int = 0) -> dict:
    key = jax.random.key(seed)
    ks = jax.random.split(key, 19)
    f32 = jnp.float32

    def w(k, shape, fan_in):
        return jax.random.normal(k, shape, f32) * fan_in ** -0.5

    def gain(k, shape):
        return 1.0 + 0.02 * jax.random.normal(k, shape, f32)

    return {
        'x': jax.random.normal(ks[0], (BATCH, SEQ, D_MODEL), f32),
        'mem': jax.random.normal(ks[1], (BATCH, N_MEM, D_MODEL), f32),
        'w_in': w(ks[2], (DEPTH, D_MODEL, 3 * MIX_WIDTH), D_MODEL),
        'w_out': w(ks[3], (DEPTH, MIX_WIDTH, D_MODEL), MIX_WIDTH),
        'g_mix': gain(ks[4], (DEPTH, D_MODEL)),
        'diff_lambda': 0.1 * jax.random.normal(ks[5], (DEPTH, 4, DIFF_QK_DIM), f32),
        'diff_subln': gain(ks[6], (DEPTH, HEAD_DIM)),
        'rel_bias_table': 0.3 * jax.random.normal(ks[7], (N_BUCKETS, N_SELF_HEADS), f32),
        'g_cross': gain(ks[8], (DEPTH, D_MODEL)),
        'g_mem': gain(ks[9], (DEPTH, D_MODEL)),
        'w_cq': w(ks[10], (DEPTH, D_MODEL, MEM_WIDTH), D_MODEL),
        'w_ckv': w(ks[11], (DEPTH, D_MODEL, 2 * MEM_WIDTH), D_MODEL),
        'w_co': w(ks[12], (DEPTH, MEM_WIDTH, D_MODEL), MEM_WIDTH),
        'g_ffn': gain(ks[13], (DEPTH, D_MODEL)),
        'w_up': w(ks[14], (DEPTH, D_MODEL, 2 * D_FF), D_MODEL),
        'conv_w': w(ks[15], (DEPTH, CONV_WIDTH, D_FF), CONV_WIDTH),
        'conv_b': 0.01 * jax.random.normal(ks[16], (DEPTH, D_FF), f32),
        'w_down': w(ks[17], (DEPTH, D_FF, D_MODEL), D_FF),
        'g_final': gain(ks[18], (D_MODEL,)),
    }


def reference(x, mem, w_in, w_out, g_mix, diff_lambda, diff_subln, rel_bias_table,
              g_cross, g_mem, w_cq, w_ckv, w_co, g_ffn, w_up, conv_w, conv_b, w_down, g_final):
    B, S, _ = x.shape
    tab_a = rel_bias_table[:, :N_HEADS_DIFF]
    tab_b = rel_bias_table[:, N_HEADS_DIFF:N_HEADS_DIFF + N_HEADS_MOBA]
    tab_c = rel_bias_table[:, N_HEADS_DIFF + N_HEADS_MOBA:]
    widths = (W_DIFF,) * 3 + (W_MOBA,) * 3 + (W_DIL,) * 3
    cuts = [int(c) for c in np.cumsum(widths)[:-1]]
    for l in range(DEPTH):
        h = rms_norm(x, g_mix[l])
        qa, ka, va, qb, kb, vb, qc, kc, vc = jnp.split(h @ w_in[l], cuts, axis=-1)
        lam_init = 0.8 - 0.6 * math.exp(-0.3 * l)
        lp = diff_lambda[l].astype(jnp.float32)
        lam = jnp.exp(jnp.sum(lp[0] * lp[1])) - jnp.exp(jnp.sum(lp[2] * lp[3])) + lam_init
        o_a = diff_attention(qa.reshape(B, S, N_HEADS_DIFF, 2, DIFF_QK_DIM),
                             ka.reshape(B, S, N_HEADS_DIFF, 2, DIFF_QK_DIM),
                             va.reshape(B, S, N_HEADS_DIFF, HEAD_DIM),
                             lam, diff_subln[l], lam_init, tab_a)
        o_b = moba_attention(qb.reshape(B, S, N_HEADS_MOBA, HEAD_DIM),
                             kb.reshape(B, S, N_HEADS_MOBA, HEAD_DIM),
                             vb.reshape(B, S, N_HEADS_MOBA, HEAD_DIM), tab_b)
        o_c = dilated_attention(qc.reshape(B, S, N_HEADS_DIL, HEAD_DIM),
                                kc.reshape(B, S, N_HEADS_DIL, HEAD_DIM),
                                vc.reshape(B, S, N_HEADS_DIL, HEAD_DIM), tab_c)
        mixed = jnp.concatenate([o_a.reshape(B, S, W_DIFF), o_b.reshape(B, S, W_MOBA),
                                 o_c.reshape(B, S, W_DIL)], axis=-1)
        x = x + mixed @ w_out[l]
        x = x + memory_cross_attention(rms_norm(x, g_cross[l]), rms_norm(mem, g_mem[l]),
                                       w_cq[l], w_ckv[l], w_co[l])
        x = x + conv_glu_ffn(rms_norm(x, g_ffn[l]), w_up[l], conv_w[l], conv_b[l], w_down[l])
    return rms_norm(x, g_final)
```

```python
import functools
import math

import numpy as np
import jax
import jax.numpy as jnp
from jax import lax
from jax.experimental import pallas as pl
from jax.experimental.pallas import tpu as pltpu

F32 = jnp.float32
BF16 = jnp.bfloat16

D_MODEL = 2048
HEAD_DIM = 128
N_HEADS_DIFF = 6
N_HEADS_MOBA = 4
N_HEADS_DIL = 6
W_DIFF = N_HEADS_DIFF * HEAD_DIM
W_MOBA = N_HEADS_MOBA * HEAD_DIM
W_DIL = N_HEADS_DIL * HEAD_DIM
MIX_WIDTH = W_DIFF + W_MOBA + W_DIL
DIFF_QK_DIM = HEAD_DIM // 2
MOBA_BLOCK = 256
MOBA_TOPK = 3
DIL_PAIRS = ((128, 1), (512, 4), (2048, 16))
DIL_BLOCK = 128
N_BUCKETS = 32
REL_MAX_DIST = 2048
N_MEM_HEADS = 4
MEM_WIDTH = N_MEM_HEADS * HEAD_DIM
D_FF = 5632
CONV_WIDTH = 3
NORM_EPS = 1e-6
NEG_INF = -1e30

_COL_QA = 0
_COL_KA = _COL_QA + N_HEADS_DIFF
_COL_VA = _COL_KA + N_HEADS_DIFF
_COL_QB = _COL_VA + N_HEADS_DIFF
_COL_KB = _COL_QB + N_HEADS_MOBA
_COL_VB = _COL_KB + N_HEADS_MOBA
_COL_QC = _COL_VB + N_HEADS_MOBA
_QKV_COLS = 3 * MIX_WIDTH // HEAD_DIM

ATTN_TILE = 256
FFN_HALO = 16
VMEM_LIMIT = 56 * 1024 * 1024

_NT = (((1,), (1,)), ((), ()))


def _params(semantics):
    return pltpu.CompilerParams(dimension_semantics=semantics, vmem_limit_bytes=VMEM_LIMIT)


def _rms_norm(xf, g):
    y = xf * lax.rsqrt(jnp.mean(xf * xf, axis=-1, keepdims=True) + NORM_EPS)
    return y * g


def _norm_matmul_kernel(x_ref, g_ref, w_ref, o_ref, hn_ref):
    @pl.when(pl.program_id(1) == 0)
    def _():
        hn_ref[...] = _rms_norm(x_ref[...], g_ref[...]).astype(BF16)

    o_ref[...] = jnp.dot(hn_ref[...], w_ref[...], preferred_element_type=F32).astype(o_ref.dtype)


def _norm_matmul(x, g, w, layer, *, tm, tn):
    t, d = x.shape
    n = w.shape[-1]
    return pl.pallas_call(
        _norm_matmul_kernel,
        grid=(t // tm, n // tn),
        in_specs=[
            pl.BlockSpec((tm, d), lambda i, j: (i, 0)),
            pl.BlockSpec((None, 1, d), lambda i, j: (layer, 0, 0)),
            pl.BlockSpec((None, d, tn), lambda i, j: (layer, 0, j)),
        ],
        out_specs=pl.BlockSpec((tm, tn), lambda i, j: (i, j)),
        out_shape=jax.ShapeDtypeStruct((t, n), BF16),
        scratch_shapes=[pltpu.VMEM((tm, d), BF16)],
        compiler_params=_params(("parallel", "arbitrary")),
        name="norm_matmul",
    )(x, g, w)


def _online_softmax_step(s, v_tile, m, l, acc):
    m_new = jnp.maximum(m, jnp.max(s, axis=-1, keepdims=True))
    alpha = jnp.exp(m - m_new)
    p = jnp.exp(s - m_new)
    l = alpha * l + jnp.sum(p, axis=-1, keepdims=True)
    acc = alpha * acc + jnp.dot(p.astype(BF16), v_tile, preferred_element_type=F32)
    return m_new, l, acc


def _diff_attn_kernel(q_ref, k_ref, v_ref, bias_ref, lam_ref, sg_ref, o_ref, *, n_bias, lam_init):
    tile = ATTN_TILE
    i = pl.program_id(2)
    q = q_ref[0]
    lane = lax.broadcasted_iota(jnp.int32, q.shape, 1)
    zero = jnp.zeros_like(q)
    qm = jnp.concatenate([jnp.where(lane < DIFF_QK_DIM, q, zero),
                          jnp.where(lane >= DIFF_QK_DIM, q, zero)], axis=0)

    def body(j, carry):
        m, l, acc = carry
        start = pl.multiple_of(j * tile, tile)
        k_tile = k_ref[0, pl.ds(start, tile), :]
        v_tile = v_ref[0, pl.ds(start, tile), :]
        s = lax.dot_general(qm, k_tile, _NT, preferred_element_type=F32)
        b_tile = bias_ref[0, jnp.minimum(i - j, n_bias - 1)]
        s = (s.reshape(2, tile, tile) + b_tile[None]).reshape(2 * tile, tile)
        return _online_softmax_step(s, v_tile, m, l, acc)

    init = (jnp.full((2 * tile, 1), NEG_INF, F32), jnp.zeros((2 * tile, 1), F32),
            jnp.zeros((2 * tile, HEAD_DIM), F32))
    _, l, acc = lax.fori_loop(0, i + 1, body, init)
    o = acc / l
    lp = lam_ref[...]
    lam = (jnp.exp(jnp.sum(lp[0:1] * lp[1:2], keepdims=True))
           - jnp.exp(jnp.sum(lp[2:3] * lp[3:4], keepdims=True)) + lam_init)
    od = o[:tile] - lam * o[tile:]
    o_ref[0] = (_rms_norm(od, sg_ref[...]) * (1.0 - lam_init)).astype(o_ref.dtype)


def _diff_attn(qkv, bias, diff_lambda, diff_subln, layer):
    b, s, _ = qkv.shape
    tile = ATTN_TILE
    n_bias = bias.shape[1]
    lam_init = 0.8 - 0.6 * math.exp(-0.3 * layer)
    kern = functools.partial(_diff_attn_kernel, n_bias=n_bias, lam_init=lam_init)
    return pl.pallas_call(
        kern,
        grid=(b, N_HEADS_DIFF, s // tile),
        in_specs=[
            pl.BlockSpec((1, tile, HEAD_DIM), lambda bi, h, i: (bi, i, _COL_QA + h)),
            pl.BlockSpec((1, s, HEAD_DIM), lambda bi, h, i: (bi, 0, _COL_KA + h)),
            pl.BlockSpec((1, s, HEAD_DIM), lambda bi, h, i: (bi, 0, _COL_VA + h)),
            pl.BlockSpec((1, n_bias, tile, tile), lambda bi, h, i: (h, 0, 0, 0)),
            pl.BlockSpec((None, 4, DIFF_QK_DIM), lambda bi, h, i: (layer, 0, 0)),
            pl.BlockSpec((None, 1, HEAD_DIM), lambda bi, h, i: (layer, 0, 0)),
        ],
        out_specs=pl.BlockSpec((1, tile, HEAD_DIM), lambda bi, h, i: (bi, i, h)),
        out_shape=jax.ShapeDtypeStruct((b, s, W_DIFF), BF16),
        compiler_params=_params(("parallel", "parallel", "arbitrary")),
        name="diff_attn",
    )(qkv, qkv, qkv, bias, diff_lambda, diff_subln)


def _moba_kernel(q_ref, k_ref, v_ref, bias_ref, o_ref, kmean_ref, *, n_bias, n_blocks):
    tile = MOBA_BLOCK
    i = pl.program_id(2)

    @pl.when(i == 0)
    def _():
        kf = k_ref[0].astype(F32).reshape(n_blocks, tile, HEAD_DIM)
        kmean_ref[...] = jnp.zeros_like(kmean_ref)
        kmean_ref[0:n_blocks, :] = jnp.mean(kf, axis=1)

    q = q_ref[0]
    gate = lax.dot_general(q.astype(F32), kmean_ref[...], _NT, preferred_element_type=F32,
                           precision=lax.Precision.HIGHEST)
    col = lax.broadcasted_iota(jnp.int32, gate.shape, 1)
    past = col < i
    gm = jnp.where(past, gate, NEG_INF)
    rank = jnp.zeros(gate.shape, jnp.int32)
    for jp in range(n_blocks):
        gj = gm[:, jp:jp + 1]
        beats = (gj > gm) | ((gj == gm) & (jp < col))
        rank = rank + beats.astype(jnp.int32)
    selected = past & (rank < MOBA_TOPK)
    q_aug = jnp.concatenate([q, jnp.where(selected, 0.0, NEG_INF).astype(BF16)], axis=1)

    own = pl.multiple_of(i * tile, tile)
    s = lax.dot_general(q, k_ref[0, pl.ds(own, tile), :], _NT, preferred_element_type=F32)
    s = s + bias_ref[0, 0]
    m = jnp.max(s, axis=-1, keepdims=True)
    p = jnp.exp(s - m)
    l = jnp.sum(p, axis=-1, keepdims=True)
    acc = jnp.dot(p.astype(BF16), v_ref[0, pl.ds(own, tile), :], preferred_element_type=F32)

    def body(j, carry):
        m, l, acc = carry
        start = pl.multiple_of(j * tile, tile)
        k_tile = k_ref[0, pl.ds(start, tile), :]
        v_tile = v_ref[0, pl.ds(start, tile), :]
        block_lane = (lax.broadcasted_iota(jnp.int32, (tile, HEAD_DIM), 1) == j).astype(BF16)
        k_aug = jnp.concatenate([k_tile, block_lane], axis=1)
        s = lax.dot_general(q_aug, k_aug, _NT, preferred_element_type=F32)
        s = s + bias_ref[0, jnp.minimum(i - j, n_bias - 1)]
        return _online_softmax_step(s, v_tile, m, l, acc)

    _, l, acc = lax.fori_loop(0, i, body, (m, l, acc))
    o_ref[0] = (acc / l).astype(o_ref.dtype)


def _moba_attn(qkv, bias):
    b, s, _ = qkv.shape
    tile = MOBA_BLOCK
    n_bias = bias.shape[1]
    n_blocks = s // tile
    assert s % tile == 0 and n_blocks <= HEAD_DIM
    kern = functools.partial(_moba_kernel, n_bias=n_bias, n_blocks=n_blocks)
    return pl.pallas_call(
        kern,
        grid=(b, N_HEADS_MOBA, s // tile),
        in_specs=[
            pl.BlockSpec((1, tile, HEAD_DIM), lambda bi, h, i: (bi, i, _COL_QB + h)),
            pl.BlockSpec((1, s, HEAD_DIM), lambda bi, h, i: (bi, 0, _COL_KB + h)),
            pl.BlockSpec((1, s, HEAD_DIM), lambda bi, h, i: (bi, 0, _COL_VB + h)),
            pl.BlockSpec((1, n_bias, tile, tile), lambda bi, h, i: (N_HEADS_DIFF + h, 0, 0, 0)),
        ],
        out_specs=pl.BlockSpec((1, tile, HEAD_DIM), lambda bi, h, i: (bi, i, h)),
        out_shape=jax.ShapeDtypeStruct((b, s, W_MOBA), BF16),
        scratch_shapes=[pltpu.VMEM((HEAD_DIM, HEAD_DIM), F32)],
        compiler_params=_params(("parallel", "parallel", "arbitrary")),
        name="moba_attn",
    )(qkv, qkv, qkv, bias)


def _dil_kernel(q_ref, k_ref, v_ref, bias_ref, o_ref, lse_ref, *, sub_len):
    blk = DIL_BLOCK

    def body(n, carry):
        q_start = pl.multiple_of(n * blk, blk)
        k_start = pl.multiple_of(jnp.maximum(n - 1, 0) * blk, blk)
        q = q_ref[0, pl.ds(q_start, blk), :]
        k = k_ref[0, pl.ds(k_start, 2 * blk), :]
        v = v_ref[0, pl.ds(k_start, 2 * blk), :]
        s = lax.dot_general(q, k, _NT, preferred_element_type=F32)
        s = s + bias_ref[0, jnp.minimum(n, 1)]
        m = jnp.max(s, axis=-1, keepdims=True)
        e = jnp.exp(s - m)
        l = jnp.sum(e, axis=-1, keepdims=True)
        o = jnp.dot(e.astype(BF16), v, preferred_element_type=F32) / l
        o_ref[0, pl.ds(q_start, blk), :] = o
        lse_ref[0, pl.ds(q_start, blk), :] = jnp.broadcast_to(m + jnp.log(l), (blk, HEAD_DIM))
        return carry

    lax.fori_loop(0, sub_len // blk, body, 0)


def _dil_attn(xq, bias, dil, row_cols, q_col):
    b, sub_len, _ = xq.shape
    assert sub_len % DIL_BLOCK == 0 and sub_len >= 2 * DIL_BLOCK
    kern = functools.partial(_dil_kernel, sub_len=sub_len)
    blk = (1, sub_len, HEAD_DIM)
    out_spec = pl.BlockSpec(blk, lambda bi, r, h: (bi, 0, r * N_HEADS_DIL + h))
    out_shape = jax.ShapeDtypeStruct((b, sub_len, dil * W_DIL), F32)
    return pl.pallas_call(
        kern,
        grid=(b, dil, N_HEADS_DIL),
        in_specs=[
            pl.BlockSpec(blk, lambda bi, r, h: (bi, 0, r * row_cols + q_col + h)),
            pl.BlockSpec(blk, lambda bi, r, h: (bi, 0, r * row_cols + q_col + N_HEADS_DIL + h)),
            pl.BlockSpec(blk, lambda bi, r, h: (bi, 0, r * row_cols + q_col + 2 * N_HEADS_DIL + h)),
            pl.BlockSpec((1, 2, DIL_BLOCK, 2 * DIL_BLOCK), lambda bi, r, h: (h, 0, 0, 0)),
        ],
        out_specs=[out_spec, out_spec],
        out_shape=[out_shape, out_shape],
        compiler_params=_params(("parallel", "parallel", "parallel")),
        name=f"dil_attn_{dil}",
    )(xq, xq, xq, bias)


def _out_proj_kernel(x_ref, a_ref, b_ref, o1_ref, l1_ref, o2_ref, l2_ref, o3_ref, l3_ref, w_ref,
                     o_ref, c_ref):
    @pl.when(pl.program_id(1) == 0)
    def _():
        l1, l2, l3 = l1_ref[...], l2_ref[...], l3_ref[...]
        m = jnp.maximum(l1, jnp.maximum(l2, l3))
        w1, w2, w3 = jnp.exp(l1 - m), jnp.exp(l2 - m), jnp.exp(l3 - m)
        merged = (w1 * o1_ref[...] + w2 * o2_ref[...] + w3 * o3_ref[...]) / (w1 + w2 + w3)
        c_ref[...] = merged.astype(BF16)

    acc = jnp.dot(a_ref[...], w_ref[0:W_DIFF, :], preferred_element_type=F32)
    acc += jnp.dot(b_ref[...], w_ref[W_DIFF:W_DIFF + W_MOBA, :], preferred_element_type=F32)
    acc += jnp.dot(c_ref[...], w_ref[W_DIFF + W_MOBA:, :], preferred_element_type=F32)
    o_ref[...] = x_ref[...] + acc


def _out_proj(x, o_a, o_b, dil_outs, w_out, layer, *, tm, tn):
    t, d = x.shape
    row = lambda width: pl.BlockSpec((tm, width), lambda i, j: (i, 0))
    dil_args, dil_specs = [], []
    for o, lse in dil_outs:
        dil_args += [o, lse]
        dil_specs += [row(W_DIL), row(W_DIL)]
    return pl.pallas_call(
        _out_proj_kernel,
        grid=(t // tm, d // tn),
        in_specs=[pl.BlockSpec((tm, tn), lambda i, j: (i, j)), row(W_DIFF), row(W_MOBA)] + dil_specs
        + [pl.BlockSpec((None, MIX_WIDTH, tn), lambda i, j: (layer, 0, j))],
        out_specs=pl.BlockSpec((tm, tn), lambda i, j: (i, j)),
        out_shape=jax.ShapeDtypeStruct((t, d), F32),
        scratch_shapes=[pltpu.VMEM((tm, W_DIL), BF16)],
        compiler_params=_params(("parallel", "arbitrary")),
        name="out_proj",
    )(x, o_a, o_b, *dil_args, w_out)


def _cross_attn_kernel(x_ref, g_ref, wq_ref, kv_ref, wo_ref, o_ref):
    x = x_ref[...]
    hn = _rms_norm(x, g_ref[...]).astype(BF16)
    q = jnp.dot(hn, wq_ref[...], preferred_element_type=F32).astype(BF16)
    kv = kv_ref[0]
    heads = []
    for h in range(N_MEM_HEADS):
        lo = h * HEAD_DIM
        s = lax.dot_general(q[:, lo:lo + HEAD_DIM], kv[:, lo:lo + HEAD_DIM], _NT,
                            preferred_element_type=F32)
        e = jnp.exp(s - jnp.max(s, axis=-1, keepdims=True))
        oh = jnp.dot(e.astype(BF16), kv[:, MEM_WIDTH + lo:MEM_WIDTH + lo + HEAD_DIM],
                     preferred_element_type=F32)
        heads.append((oh / jnp.sum(e, axis=-1, keepdims=True)).astype(BF16))
    o = jnp.concatenate(heads, axis=1)
    o_ref[...] = x + jnp.dot(o, wo_ref[...], preferred_element_type=F32)


def _cross_attn(x, g, w_cq, kv, w_co, layer, seq_len, *, tm):
    t, d = x.shape
    n_mem = kv.shape[1]
    tiles_per_batch = seq_len // tm
    return pl.pallas_call(
        _cross_attn_kernel,
        grid=(t // tm,),
        in_specs=[
            pl.BlockSpec((tm, d), lambda i: (i, 0)),
            pl.BlockSpec((None, 1, d), lambda i: (layer, 0, 0)),
            pl.BlockSpec((None, d, MEM_WIDTH), lambda i: (layer, 0, 0)),
            pl.BlockSpec((1, n_mem, 2 * MEM_WIDTH), lambda i: (i // tiles_per_batch, 0, 0)),
            pl.BlockSpec((None, MEM_WIDTH, d), lambda i: (layer, 0, 0)),
        ],
        out_specs=pl.BlockSpec((tm, d), lambda i: (i, 0)),
        out_shape=jax.ShapeDtypeStruct((t, d), F32),
        compiler_params=_params(("parallel",)),
        name="cross_attn",
    )(x, g, w_cq, kv, w_co)


def _ffn_kernel(x_ref, xp_ref, g_ref, wg_ref, wu_ref, cw_ref, cb_ref, wd_ref, gf_ref, o_ref, hn_ref,
                *, tm, tiles_per_batch, final_norm):
    i = pl.program_id(0)
    f = pl.program_id(1)

    @pl.when(f == 0)
    def _():
        g = g_ref[...]
        hn_ref[FFN_HALO:, :] = _rms_norm(x_ref[...], g).astype(BF16)
        prev = _rms_norm(xp_ref[...], g)
        prev = jnp.where(i % tiles_per_batch == 0, 0.0, prev)
        hn_ref[0:FFN_HALO, :] = prev.astype(BF16)
        o_ref[...] = x_ref[...]

    gate = jnp.dot(hn_ref[...], wg_ref[...], preferred_element_type=F32)
    up = jnp.dot(hn_ref[FFN_HALO:, :], wu_ref[...], preferred_element_type=F32)
    cw = cw_ref[...]
    conv = (cw[0:1] * gate[FFN_HALO - 2:FFN_HALO - 2 + tm]
            + cw[1:2] * gate[FFN_HALO - 1:FFN_HALO - 1 + tm]
            + cw[2:3] * gate[FFN_HALO:]) + cb_ref[...]
    act = conv * jax.nn.sigmoid(conv) * up
    o_ref[...] += jnp.dot(act.astype(BF16), wd_ref[...], preferred_element_type=F32)

    if final_norm:
        @pl.when(f == pl.num_programs(1) - 1)
        def _():
            o_ref[...] = _rms_norm(o_ref[...], gf_ref[...])


def _conv_ffn(x, g, w_up, conv_w, conv_b, w_down, g_final, layer, seq_len, *, tm, tf, final_norm):
    t, d = x.shape
    n_f = D_FF // tf
    tiles_per_batch = seq_len // tm
    halo_blocks = tm // FFN_HALO
    kern = functools.partial(_ffn_kernel, tm=tm, tiles_per_batch=tiles_per_batch, final_norm=final_norm)
    return pl.pallas_call(
        kern,
        grid=(t // tm, n_f),
        in_specs=[
            pl.BlockSpec((tm, d), lambda i, f: (i, 0)),
            pl.BlockSpec((FFN_HALO, d), lambda i, f: (jnp.maximum(i * halo_blocks - 1, 0), 0)),
            pl.BlockSpec((None, 1, d), lambda i, f: (layer, 0, 0)),
            pl.BlockSpec((None, d, tf), lambda i, f: (layer, 0, f)),
            pl.BlockSpec((None, d, tf), lambda i, f: (layer, 0, n_f + f)),
            pl.BlockSpec((None, CONV_WIDTH, tf), lambda i, f: (layer, 0, f)),
            pl.BlockSpec((None, 1, tf), lambda i, f: (layer, 0, f)),
            pl.BlockSpec((None, tf, d), lambda i, f: (layer, f, 0)),
            pl.BlockSpec((1, d), lambda i, f: (0, 0)),
        ],
        out_specs=pl.BlockSpec((tm, d), lambda i, f: (i, 0)),
        out_shape=jax.ShapeDtypeStruct((t, d), F32),
        scratch_shapes=[pltpu.VMEM((FFN_HALO + tm, d), BF16)],
        compiler_params=_params(("parallel", "arbitrary")),
        name="conv_ffn",
    )(x, x, g, w_up, w_up, conv_w, conv_b, w_down, g_final)


def _t5_bucket(dist):
    n = jnp.maximum(dist, 0)
    max_exact = N_BUCKETS // 2
    nf = jnp.maximum(n, 1).astype(F32)
    log_ratio = jnp.log(nf / max_exact) / math.log(REL_MAX_DIST / max_exact)
    large = max_exact + (log_ratio * (N_BUCKETS - max_exact)).astype(jnp.int32)
    large = jnp.minimum(large, N_BUCKETS - 1)
    return jnp.where(n < max_exact, n, large)


def _causal_bias_tiles(table, tile):
    n_tiles = -(-(REL_MAX_DIST + tile - 1) // tile) + 1
    dist = np.arange(-(tile - 1), n_tiles * tile, dtype=np.int32)
    vec = jnp.where(dist[None, :] >= 0, table.astype(F32)[_t5_bucket(jnp.asarray(dist))].T, NEG_INF)
    qi = np.arange(tile)[:, None]
    ki = np.arange(tile)[None, :]
    idx = (np.arange(n_tiles)[:, None, None] * tile + qi - ki) + (tile - 1)
    return vec[:, idx]


def _dilated_bias_tiles(table, window, dil):
    blk = DIL_BLOCK
    steps = window // dil
    qi = np.arange(blk)[:, None]
    kk = np.arange(2 * blk)[None, :]
    j_first = qi - kk
    j_general = qi - (kk - blk)
    j = np.stack([j_first, j_general])
    valid = (j >= 0) & (j <= steps)
    bias = table.astype(F32)[_t5_bucket(jnp.asarray(np.maximum(j, 0) * dil))]
    bias = jnp.moveaxis(bias, -1, 0)
    return jnp.where(valid[None], bias, NEG_INF)


def kernel(x, mem, w_in, w_out, g_mix, diff_lambda, diff_subln, rel_bias_table, g_cross, g_mem,
           w_cq, w_ckv, w_co, g_ffn, w_up, conv_w, conv_b, w_down, g_final):
    b, s, d = x.shape
    depth = w_in.shape[0]
    n_mem = mem.shape[1]
    t = b * s

    col_scale = np.ones((3 * MIX_WIDTH,), np.float32)
    col_scale[_COL_QA * HEAD_DIM:_COL_KA * HEAD_DIM] = DIFF_QK_DIM ** -0.5
    col_scale[_COL_QB * HEAD_DIM:_COL_KB * HEAD_DIM] = HEAD_DIM ** -0.5
    col_scale[_COL_QC * HEAD_DIM:(_COL_QC + N_HEADS_DIL) * HEAD_DIM] = HEAD_DIM ** -0.5
    w_in_b = (w_in * col_scale).astype(BF16)
    w_out_b = w_out.astype(BF16)
    w_cq_b = (w_cq * HEAD_DIM ** -0.5).astype(BF16)
    w_ckv_b = w_ckv.astype(BF16)
    w_co_b = w_co.astype(BF16)
    w_up_b = w_up.astype(BF16)
    w_down_b = w_down.astype(BF16)

    g_mix3 = g_mix.reshape(depth, 1, d)
    g_cross3 = g_cross.reshape(depth, 1, d)
    g_mem3 = g_mem.reshape(depth, 1, d)
    g_ffn3 = g_ffn.reshape(depth, 1, d)
    subln3 = diff_subln.reshape(depth, 1, HEAD_DIM)
    conv_b3 = conv_b.reshape(depth, 1, D_FF)
    g_final2 = g_final.reshape(1, d)

    n_ab = N_HEADS_DIFF + N_HEADS_MOBA
    bias_ab = _causal_bias_tiles(rel_bias_table[:, :n_ab], ATTN_TILE)
    bias_c = [_dilated_bias_tiles(rel_bias_table[:, n_ab:], window, dil) for window, dil in DIL_PAIRS]

    xf = x.reshape(t, d)
    memf = mem.reshape(b * n_mem, d)
    c_cols = 3 * N_HEADS_DIL

    for layer in range(depth):
        qkv = _norm_matmul(xf, g_mix3, w_in_b, layer, tm=1024, tn=512).reshape(b, s, 3 * MIX_WIDTH)
        o_a = _diff_attn(qkv, bias_ab, diff_lambda, subln3, layer)
        o_b = _moba_attn(qkv, bias_ab)
        dil_outs = []
        for (window, dil), bias in zip(DIL_PAIRS, bias_c):
            if dil == 1:
                o, lse = _dil_attn(qkv, bias, 1, _QKV_COLS, _COL_QC)
            else:
                xq = qkv[:, :, _COL_QC * HEAD_DIM:].reshape(b, s // dil, dil * c_cols * HEAD_DIM)
                o, lse = _dil_attn(xq, bias, dil, c_cols, 0)
            dil_outs.append((o.reshape(t, W_DIL), lse.reshape(t, W_DIL)))
        xf = _out_proj(xf, o_a.reshape(t, W_DIFF), o_b.reshape(t, W_MOBA), dil_outs, w_out_b, layer,
                       tm=512, tn=1024)
        kv = _norm_matmul(memf, g_mem3, w_ckv_b, layer, tm=b * n_mem, tn=512)
        xf = _cross_attn(xf, g_cross3, w_cq_b, kv.reshape(b, n_mem, 2 * MEM_WIDTH), w_co_b, layer, s,
                         tm=512)
        xf = _conv_ffn(xf, g_ffn3, w_up_b, conv_w, conv_b3, w_down_b, g_final2, layer, s,
                       tm=512, tf=512, final_norm=(layer == depth - 1))
    return xf.reshape(b, s, d)
```

```python
import functools
import math

import numpy as np
import jax
import jax.numpy as jnp
from jax import lax
from jax.experimental import pallas as pl
from jax.experimental.pallas import tpu as pltpu

F32 = jnp.float32
BF16 = jnp.bfloat16

D_MODEL = 2048
HEAD_DIM = 128
N_HEADS_DIFF = 6
N_HEADS_MOBA = 4
N_HEADS_DIL = 6
W_DIFF = N_HEADS_DIFF * HEAD_DIM
W_MOBA = N_HEADS_MOBA * HEAD_DIM
W_DIL = N_HEADS_DIL * HEAD_DIM
MIX_WIDTH = W_DIFF + W_MOBA + W_DIL
DIFF_QK_DIM = HEAD_DIM // 2
MOBA_BLOCK = 256
MOBA_TOPK = 3
DIL_PAIRS = ((128, 1), (512, 4), (2048, 16))
DIL_BLOCK = 128
N_BUCKETS = 32
REL_MAX_DIST = 2048
N_MEM_HEADS = 4
MEM_WIDTH = N_MEM_HEADS * HEAD_DIM
D_FF = 5632
CONV_WIDTH = 3
NORM_EPS = 1e-6
NEG_INF = -1e30

_COL_QA = 0
_COL_KA = _COL_QA + N_HEADS_DIFF
_COL_VA = _COL_KA + N_HEADS_DIFF
_COL_QB = _COL_VA + N_HEADS_DIFF
_COL_KB = _COL_QB + N_HEADS_MOBA
_COL_VB = _COL_KB + N_HEADS_MOBA
_COL_QC = _COL_VB + N_HEADS_MOBA
_QKV_COLS = 3 * MIX_WIDTH // HEAD_DIM

ATTN_TILE = 256
KV_TILE = 1024
VT_ROWS = HEAD_DIM + 16
LOG2E = math.log2(math.e)
FFN_HALO = 16
VMEM_LIMIT = 56 * 1024 * 1024

_NT = (((1,), (1,)), ((), ()))


def _params(semantics):
    return pltpu.CompilerParams(dimension_semantics=semantics, vmem_limit_bytes=VMEM_LIMIT)


def _rms_norm(xf, g):
    y = xf * lax.rsqrt(jnp.mean(xf * xf, axis=-1, keepdims=True) + NORM_EPS)
    return y * g


def _norm_matmul_kernel(x_ref, g_ref, w_ref, o_ref, hn_ref):
    @pl.when(pl.program_id(1) == 0)
    def _():
        hn_ref[...] = _rms_norm(x_ref[...], g_ref[...]).astype(BF16)

    o_ref[...] = jnp.dot(hn_ref[...], w_ref[...], preferred_element_type=F32).astype(o_ref.dtype)


def _norm_matmul(x, g, w, layer, *, tm, tn):
    t, d = x.shape
    n = w.shape[-1]
    return pl.pallas_call(
        _norm_matmul_kernel,
        grid=(t // tm, n // tn),
        in_specs=[
            pl.BlockSpec((tm, d), lambda i, j: (i, 0)),
            pl.BlockSpec((None, 1, d), lambda i, j: (layer, 0, 0)),
            pl.BlockSpec((None, d, tn), lambda i, j: (layer, 0, j)),
        ],
        out_specs=pl.BlockSpec((tm, tn), lambda i, j: (i, j)),
        out_shape=jax.ShapeDtypeStruct((t, n), BF16),
        scratch_shapes=[pltpu.VMEM((tm, d), BF16)],
        compiler_params=_params(("parallel", "arbitrary")),
        name="norm_matmul",
    )(x, g, w)


def _flash_step(s_t, vt_tile, m, acc):
    m_new = jnp.maximum(m, jnp.max(s_t, axis=0, keepdims=True))
    alpha = jnp.exp2(m - m_new)
    p = jnp.exp2(s_t - m_new)
    acc = alpha * acc + jnp.dot(vt_tile, p.astype(BF16), preferred_element_type=F32)
    return m_new, acc


def _bias_rows(bias_ref, d, n_bias):
    tiles = []
    for c in range(KV_TILE // ATTN_TILE):
        dc = d - c
        tiles.append(bias_ref[0, jnp.where(dc < 0, n_bias, jnp.minimum(dc, n_bias - 1))])
    return jnp.concatenate(tiles, axis=0)


def _transpose_values(v_ref, vt_ref, seq_len):
    def body(c, carry):
        start = pl.multiple_of(c * KV_TILE, KV_TILE)
        vt_ref[0:HEAD_DIM, pl.ds(start, KV_TILE)] = (
            v_ref[0, pl.ds(start, KV_TILE), :].astype(F32).T.astype(BF16))
        return carry
    lax.fori_loop(0, seq_len // KV_TILE, body, 0)
    vt_ref[HEAD_DIM:, :] = jnp.ones((VT_ROWS - HEAD_DIM, seq_len), BF16)


def _diff_attn_kernel(q_ref, k_ref, v_ref, bias_ref, lam_ref, sg_ref, o_ref, vt_ref,
                      *, n_bias, lam_init, seq_len):
    tq, tk = ATTN_TILE, KV_TILE
    sub = tk // tq
    i = pl.program_id(2)

    @pl.when(i == 0)
    def _():
        _transpose_values(v_ref, vt_ref, seq_len)

    q_t = q_ref[0].astype(F32).T
    row = lax.broadcasted_iota(jnp.int32, q_t.shape, 0)
    qm_t = jnp.concatenate([jnp.where(row < DIFF_QK_DIM, q_t, 0.0),
                            jnp.where(row >= DIFF_QK_DIM, q_t, 0.0)], axis=1).astype(BF16)

    def body(j, carry):
        start = pl.multiple_of(j * tk, tk)
        s_t = jnp.dot(k_ref[0, pl.ds(start, tk), :], qm_t, preferred_element_type=F32)
        b_t = _bias_rows(bias_ref, i - j * sub, n_bias)
        s_t = s_t + jnp.concatenate([b_t, b_t], axis=1)
        return _flash_step(s_t, vt_ref[:, pl.ds(start, tk)], *carry)

    init = (jnp.full((1, 2 * tq), NEG_INF, F32), jnp.zeros((VT_ROWS, 2 * tq), F32))
    _, acc = lax.fori_loop(0, i // sub + 1, body, init)
    o_t = acc[:HEAD_DIM] / acc[HEAD_DIM:HEAD_DIM + 1]
    lp = lam_ref[...]
    lam = (jnp.exp(jnp.sum(lp[0:1] * lp[1:2], keepdims=True))
           - jnp.exp(jnp.sum(lp[2:3] * lp[3:4], keepdims=True)) + lam_init)
    od_t = o_t[:, :tq] - lam * o_t[:, tq:]
    y_t = od_t * lax.rsqrt(jnp.mean(od_t * od_t, axis=0, keepdims=True) + NORM_EPS) * sg_ref[...]
    o_ref[0] = (y_t * (1.0 - lam_init)).T.astype(o_ref.dtype)


def _diff_attn(qkv, bias_t, diff_lambda, diff_subln_col, layer):
    b, s, _ = qkv.shape
    tile = ATTN_TILE
    n_bias = bias_t.shape[1] - 1
    assert s % KV_TILE == 0
    lam_init = 0.8 - 0.6 * math.exp(-0.3 * layer)
    kern = functools.partial(_diff_attn_kernel, n_bias=n_bias, lam_init=lam_init, seq_len=s)
    return pl.pallas_call(
        kern,
        grid=(b, N_HEADS_DIFF, s // tile),
        in_specs=[
            pl.BlockSpec((1, tile, HEAD_DIM), lambda bi, h, i: (bi, i, _COL_QA + h)),
            pl.BlockSpec((1, s, HEAD_DIM), lambda bi, h, i: (bi, 0, _COL_KA + h)),
            pl.BlockSpec((1, s, HEAD_DIM), lambda bi, h, i: (bi, 0, _COL_VA + h)),
            pl.BlockSpec((1, n_bias + 1, tile, tile), lambda bi, h, i: (h, 0, 0, 0)),
            pl.BlockSpec((None, 4, DIFF_QK_DIM), lambda bi, h, i: (layer, 0, 0)),
            pl.BlockSpec((None, HEAD_DIM, 1), lambda bi, h, i: (layer, 0, 0)),
        ],
        out_specs=pl.BlockSpec((1, tile, HEAD_DIM), lambda bi, h, i: (bi, i, h)),
        out_shape=jax.ShapeDtypeStruct((b, s, W_DIFF), BF16),
        scratch_shapes=[pltpu.VMEM((VT_ROWS, s), BF16)],
        compiler_params=_params(("parallel", "parallel", "arbitrary")),
        name="diff_attn",
    )(qkv, qkv, qkv, bias_t, diff_lambda, diff_subln_col)


def _moba_kernel(q_ref, k_ref, v_ref, lanes_ref, bias_ref, o_ref, vt_ref, kmean_ref,
                 *, n_bias, n_blocks, seq_len):
    tq, tk = MOBA_BLOCK, KV_TILE
    sub = tk // tq
    i = pl.program_id(2)

    @pl.when(i == 0)
    def _():
        _transpose_values(v_ref, vt_ref, seq_len)
        kf = k_ref[0].astype(F32).reshape(n_blocks, tq, HEAD_DIM)
        kmean_ref[...] = jnp.mean(kf, axis=1)

    q_t = q_ref[0].astype(F32).T
    gate = jnp.dot(kmean_ref[...], q_t, preferred_element_type=F32,
                   precision=lax.Precision.HIGHEST)
    blk = lax.broadcasted_iota(jnp.int32, gate.shape, 0)
    past = blk < i
    gm = jnp.where(past, gate, NEG_INF)
    rank = jnp.zeros(gate.shape, jnp.int32)
    for jp in range(n_blocks):
        gj = gm[jp:jp + 1, :]
        beats = (gj > gm) | ((gj == gm) & (jp < blk))
        rank = rank + beats.astype(jnp.int32)
    allowed = (past & (rank < MOBA_TOPK)) | (blk == i)
    mask_t = jnp.where(allowed, 0.0, NEG_INF)
    mask_t = jnp.concatenate([mask_t, jnp.zeros((HEAD_DIM - n_blocks, tq), F32)], axis=0)
    q_aug_t = jnp.concatenate([q_t.astype(BF16), mask_t.astype(BF16)], axis=0)

    def body(j, carry):
        start = pl.multiple_of(j * tk, tk)
        k_aug = jnp.concatenate([k_ref[0, pl.ds(start, tk), :], lanes_ref[pl.ds(start, tk), :]], axis=1)
        s_t = jnp.dot(k_aug, q_aug_t, preferred_element_type=F32)
        s_t = s_t + _bias_rows(bias_ref, i - j * sub, n_bias)
        return _flash_step(s_t, vt_ref[:, pl.ds(start, tk)], *carry)

    init = (jnp.full((1, tq), NEG_INF, F32), jnp.zeros((VT_ROWS, tq), F32))
    _, acc = lax.fori_loop(0, i // sub + 1, body, init)
    o_ref[0] = (acc[:HEAD_DIM] / acc[HEAD_DIM:HEAD_DIM + 1]).T.astype(o_ref.dtype)


def _moba_attn(qkv, bias_t):
    b, s, _ = qkv.shape
    tile = MOBA_BLOCK
    n_bias = bias_t.shape[1] - 1
    n_blocks = s // tile
    assert s % KV_TILE == 0 and n_blocks <= HEAD_DIM and n_blocks % 8 == 0
    block_lanes = np.zeros((s, HEAD_DIM), np.float32)
    block_lanes[np.arange(s), np.arange(s) // tile] = 1.0
    kern = functools.partial(_moba_kernel, n_bias=n_bias, n_blocks=n_blocks, seq_len=s)
    return pl.pallas_call(
        kern,
        grid=(b, N_HEADS_MOBA, s // tile),
        in_specs=[
            pl.BlockSpec((1, tile, HEAD_DIM), lambda bi, h, i: (bi, i, _COL_QB + h)),
            pl.BlockSpec((1, s, HEAD_DIM), lambda bi, h, i: (bi, 0, _COL_KB + h)),
            pl.BlockSpec((1, s, HEAD_DIM), lambda bi, h, i: (bi, 0, _COL_VB + h)),
            pl.BlockSpec((s, HEAD_DIM), lambda bi, h, i: (0, 0)),
            pl.BlockSpec((1, n_bias + 1, tile, tile), lambda bi, h, i: (N_HEADS_DIFF + h, 0, 0, 0)),
        ],
        out_specs=pl.BlockSpec((1, tile, HEAD_DIM), lambda bi, h, i: (bi, i, h)),
        out_shape=jax.ShapeDtypeStruct((b, s, W_MOBA), BF16),
        scratch_shapes=[pltpu.VMEM((VT_ROWS, s), BF16), pltpu.VMEM((n_blocks, HEAD_DIM), F32)],
        compiler_params=_params(("parallel", "parallel", "arbitrary")),
        name="moba_attn",
    )(qkv, qkv, qkv, jnp.asarray(block_lanes, BF16), bias_t)


def _dil_kernel(q_ref, k_ref, v_ref, bias_ref, o_ref, lse_ref, *, sub_len):
    blk = DIL_BLOCK

    def body(n, carry):
        q_start = pl.multiple_of(n * blk, blk)
        k_start = pl.multiple_of(jnp.maximum(n - 1, 0) * blk, blk)
        q = q_ref[0, pl.ds(q_start, blk), :]
        k = k_ref[0, pl.ds(k_start, 2 * blk), :]
        v = v_ref[0, pl.ds(k_start, 2 * blk), :]
        s = lax.dot_general(q, k, _NT, preferred_element_type=F32)
        s = s + bias_ref[0, jnp.minimum(n, 1)]
        m = jnp.max(s, axis=-1, keepdims=True)
        e = jnp.exp(s - m)
        l = jnp.sum(e, axis=-1, keepdims=True)
        o = jnp.dot(e.astype(BF16), v, preferred_element_type=F32) / l
        o_ref[0, pl.ds(q_start, blk), :] = o
        lse_ref[0, pl.ds(q_start, blk), :] = jnp.broadcast_to(m + jnp.log(l), (blk, HEAD_DIM))
        return carry

    lax.fori_loop(0, sub_len // blk, body, 0)


def _dil_attn(xq, bias, dil, row_cols, q_col):
    b, sub_len, _ = xq.shape
    assert sub_len % DIL_BLOCK == 0 and sub_len >= 2 * DIL_BLOCK
    kern = functools.partial(_dil_kernel, sub_len=sub_len)
    blk = (1, sub_len, HEAD_DIM)
    out_spec = pl.BlockSpec(blk, lambda bi, r, h: (bi, 0, r * N_HEADS_DIL + h))
    out_shape = jax.ShapeDtypeStruct((b, sub_len, dil * W_DIL), F32)
    return pl.pallas_call(
        kern,
        grid=(b, dil, N_HEADS_DIL),
        in_specs=[
            pl.BlockSpec(blk, lambda bi, r, h: (bi, 0, r * row_cols + q_col + h)),
            pl.BlockSpec(blk, lambda bi, r, h: (bi, 0, r * row_cols + q_col + N_HEADS_DIL + h)),
            pl.BlockSpec(blk, lambda bi, r, h: (bi, 0, r * row_cols + q_col + 2 * N_HEADS_DIL + h)),
            pl.BlockSpec((1, 2, DIL_BLOCK, 2 * DIL_BLOCK), lambda bi, r, h: (h, 0, 0, 0)),
        ],
        out_specs=[out_spec, out_spec],
        out_shape=[out_shape, out_shape],
        compiler_params=_params(("parallel", "parallel", "parallel")),
        name=f"dil_attn_{dil}",
    )(xq, xq, xq, bias)


def _out_proj_kernel(x_ref, a_ref, b_ref, o1_ref, l1_ref, o2_ref, l2_ref, o3_ref, l3_ref, w_ref,
                     o_ref, c_ref):
    @pl.when(pl.program_id(1) == 0)
    def _():
        l1, l2, l3 = l1_ref[...], l2_ref[...], l3_ref[...]
        m = jnp.maximum(l1, jnp.maximum(l2, l3))
        w1, w2, w3 = jnp.exp(l1 - m), jnp.exp(l2 - m), jnp.exp(l3 - m)
        merged = (w1 * o1_ref[...] + w2 * o2_ref[...] + w3 * o3_ref[...]) / (w1 + w2 + w3)
        c_ref[...] = merged.astype(BF16)

    acc = jnp.dot(a_ref[...], w_ref[0:W_DIFF, :], preferred_element_type=F32)
    acc += jnp.dot(b_ref[...], w_ref[W_DIFF:W_DIFF + W_MOBA, :], preferred_element_type=F32)
    acc += jnp.dot(c_ref[...], w_ref[W_DIFF + W_MOBA:, :], preferred_element_type=F32)
    o_ref[...] = x_ref[...] + acc


def _out_proj(x, o_a, o_b, dil_outs, w_out, layer, *, tm, tn):
    t, d = x.shape
    row = lambda width: pl.BlockSpec((tm, width), lambda i, j: (i, 0))
    dil_args, dil_specs = [], []
    for o, lse in dil_outs:
        dil_args += [o, lse]
        dil_specs += [row(W_DIL), row(W_DIL)]
    return pl.pallas_call(
        _out_proj_kernel,
        grid=(t // tm, d // tn),
        in_specs=[pl.BlockSpec((tm, tn), lambda i, j: (i, j)), row(W_DIFF), row(W_MOBA)] + dil_specs
        + [pl.BlockSpec((None, MIX_WIDTH, tn), lambda i, j: (layer, 0, j))],
        out_specs=pl.BlockSpec((tm, tn), lambda i, j: (i, j)),
        out_shape=jax.ShapeDtypeStruct((t, d), F32),
        scratch_shapes=[pltpu.VMEM((tm, W_DIL), BF16)],
        compiler_params=_params(("parallel", "arbitrary")),
        name="out_proj",
    )(x, o_a, o_b, *dil_args, w_out)


def _cross_attn_kernel(x_ref, g_ref, wq_ref, kv_ref, wo_ref, o_ref):
    x = x_ref[...]
    hn = _rms_norm(x, g_ref[...]).astype(BF16)
    q = jnp.dot(hn, wq_ref[...], preferred_element_type=F32).astype(BF16)
    kv = kv_ref[0]
    heads = []
    for h in range(N_MEM_HEADS):
        lo = h * HEAD_DIM
        s = lax.dot_general(q[:, lo:lo + HEAD_DIM], kv[:, lo:lo + HEAD_DIM], _NT,
                            preferred_element_type=F32)
        e = jnp.exp(s - jnp.max(s, axis=-1, keepdims=True))
        oh = jnp.dot(e.astype(BF16), kv[:, MEM_WIDTH + lo:MEM_WIDTH + lo + HEAD_DIM],
                     preferred_element_type=F32)
        heads.append((oh / jnp.sum(e, axis=-1, keepdims=True)).astype(BF16))
    o = jnp.concatenate(heads, axis=1)
    o_ref[...] = x + jnp.dot(o, wo_ref[...], preferred_element_type=F32)


def _cross_attn(x, g, w_cq, kv, w_co, layer, seq_len, *, tm):
    t, d = x.shape
    n_mem = kv.shape[1]
    tiles_per_batch = seq_len // tm
    return pl.pallas_call(
        _cross_attn_kernel,
        grid=(t // tm,),
        in_specs=[
            pl.BlockSpec((tm, d), lambda i: (i, 0)),
            pl.BlockSpec((None, 1, d), lambda i: (layer, 0, 0)),
            pl.BlockSpec((None, d, MEM_WIDTH), lambda i: (layer, 0, 0)),
            pl.BlockSpec((1, n_mem, 2 * MEM_WIDTH), lambda i: (i // tiles_per_batch, 0, 0)),
            pl.BlockSpec((None, MEM_WIDTH, d), lambda i: (layer, 0, 0)),
        ],
        out_specs=pl.BlockSpec((tm, d), lambda i: (i, 0)),
        out_shape=jax.ShapeDtypeStruct((t, d), F32),
        compiler_params=_params(("parallel",)),
        name="cross_attn",
    )(x, g, w_cq, kv, w_co)


def _ffn_kernel(x_ref, xp_ref, g_ref, wg_ref, wu_ref, cw_ref, cb_ref, wd_ref, gf_ref, o_ref, hn_ref,
                *, tm, tiles_per_batch, final_norm):
    i = pl.program_id(0)
    f = pl.program_id(1)

    @pl.when(f == 0)
    def _():
        g = g_ref[...]
        hn_ref[FFN_HALO:, :] = _rms_norm(x_ref[...], g).astype(BF16)
        prev = _rms_norm(xp_ref[...], g)
        prev = jnp.where(i % tiles_per_batch == 0, 0.0, prev)
        hn_ref[0:FFN_HALO, :] = prev.astype(BF16)
        o_ref[...] = x_ref[...]

    gate = jnp.dot(hn_ref[...], wg_ref[...], preferred_element_type=F32)
    up = jnp.dot(hn_ref[FFN_HALO:, :], wu_ref[...], preferred_element_type=F32)
    cw = cw_ref[...]
    conv = (cw[0:1] * gate[FFN_HALO - 2:FFN_HALO - 2 + tm]
            + cw[1:2] * gate[FFN_HALO - 1:FFN_HALO - 1 + tm]
            + cw[2:3] * gate[FFN_HALO:]) + cb_ref[...]
    act = conv * jax.nn.sigmoid(conv) * up
    o_ref[...] += jnp.dot(act.astype(BF16), wd_ref[...], preferred_element_type=F32)

    if final_norm:
        @pl.when(f == pl.num_programs(1) - 1)
        def _():
            o_ref[...] = _rms_norm(o_ref[...], gf_ref[...])


def _conv_ffn(x, g, w_up, conv_w, conv_b, w_down, g_final, layer, seq_len, *, tm, tf, final_norm):
    t, d = x.shape
    n_f = D_FF // tf
    tiles_per_batch = seq_len // tm
    halo_blocks = tm // FFN_HALO
    kern = functools.partial(_ffn_kernel, tm=tm, tiles_per_batch=tiles_per_batch, final_norm=final_norm)
    return pl.pallas_call(
        kern,
        grid=(t // tm, n_f),
        in_specs=[
            pl.BlockSpec((tm, d), lambda i, f: (i, 0)),
            pl.BlockSpec((FFN_HALO, d), lambda i, f: (jnp.maximum(i * halo_blocks - 1, 0), 0)),
            pl.BlockSpec((None, 1, d), lambda i, f: (layer, 0, 0)),
            pl.BlockSpec((None, d, tf), lambda i, f: (layer, 0, f)),
            pl.BlockSpec((None, d, tf), lambda i, f: (layer, 0, n_f + f)),
            pl.BlockSpec((None, CONV_WIDTH, tf), lambda i, f: (layer, 0, f)),
            pl.BlockSpec((None, 1, tf), lambda i, f: (layer, 0, f)),
            pl.BlockSpec((None, tf, d), lambda i, f: (layer, f, 0)),
            pl.BlockSpec((1, d), lambda i, f: (0, 0)),
        ],
        out_specs=pl.BlockSpec((tm, d), lambda i, f: (i, 0)),
        out_shape=jax.ShapeDtypeStruct((t, d), F32),
        scratch_shapes=[pltpu.VMEM((FFN_HALO + tm, d), BF16)],
        compiler_params=_params(("parallel", "arbitrary")),
        name="conv_ffn",
    )(x, x, g, w_up, w_up, conv_w, conv_b, w_down, g_final)


def _t5_bucket(dist):
    n = jnp.maximum(dist, 0)
    max_exact = N_BUCKETS // 2
    nf = jnp.maximum(n, 1).astype(F32)
    log_ratio = jnp.log(nf / max_exact) / math.log(REL_MAX_DIST / max_exact)
    large = max_exact + (log_ratio * (N_BUCKETS - max_exact)).astype(jnp.int32)
    large = jnp.minimum(large, N_BUCKETS - 1)
    return jnp.where(n < max_exact, n, large)


def _toeplitz(w, rows, cols, offset):
    length = cols + offset + 1
    assert w.shape[-1] == length and offset >= rows - 1
    lead = w.shape[:-1]
    flat = jnp.broadcast_to(w[..., None, :], lead + (rows, length)).reshape(lead + (rows * length,))
    shifted = flat[..., :rows * (length - 1)].reshape(lead + (rows, length - 1))
    return shifted[..., offset:offset + cols]


def _causal_bias_tiles(table, tile):
    n_tiles = -(-(REL_MAX_DIST + tile - 1) // tile) + 1
    dist = np.arange(-(tile - 1), n_tiles * tile + 1, dtype=np.int32)
    vec = jnp.where(dist[None, :] >= 0, table.astype(F32)[_t5_bucket(jnp.asarray(dist))].T, NEG_INF)
    tiles = _toeplitz(vec, tile, n_tiles * tile, tile - 1)
    tiles = tiles.reshape(-1, tile, n_tiles, tile).transpose(0, 2, 1, 3)
    return jnp.concatenate([tiles, jnp.full((tiles.shape[0], 1, tile, tile), NEG_INF, F32)], axis=1)


def _dilated_bias_tiles(table, window, dil):
    blk = DIL_BLOCK
    steps = window // dil
    j = np.arange(-(2 * blk - 1), 2 * blk + 1, dtype=np.int32)
    valid = (j >= 0) & (j <= steps)
    bias = table.astype(F32)[_t5_bucket(jnp.asarray(np.maximum(j, 0) * dil))].T
    vec = jnp.where(valid[None, :], bias, NEG_INF)
    tiles = _toeplitz(vec, 2 * blk, 2 * blk, 2 * blk - 1)
    return tiles.reshape(-1, 2 * blk, 2, blk).transpose(0, 2, 3, 1)


def kernel(x, mem, w_in, w_out, g_mix, diff_lambda, diff_subln, rel_bias_table, g_cross, g_mem,
           w_cq, w_ckv, w_co, g_ffn, w_up, conv_w, conv_b, w_down, g_final):
    b, s, d = x.shape
    depth = w_in.shape[0]
    n_mem = mem.shape[1]
    t = b * s

    col_scale = np.ones((3 * MIX_WIDTH,), np.float32)
    col_scale[_COL_QA * HEAD_DIM:_COL_KA * HEAD_DIM] = DIFF_QK_DIM ** -0.5 * LOG2E
    col_scale[_COL_QB * HEAD_DIM:_COL_KB * HEAD_DIM] = HEAD_DIM ** -0.5 * LOG2E
    col_scale[_COL_QC * HEAD_DIM:(_COL_QC + N_HEADS_DIL) * HEAD_DIM] = HEAD_DIM ** -0.5
    w_in_b = (w_in * col_scale).astype(BF16)
    w_out_b = w_out.astype(BF16)
    w_cq_b = (w_cq * HEAD_DIM ** -0.5).astype(BF16)
    w_ckv_b = w_ckv.astype(BF16)
    w_co_b = w_co.astype(BF16)
    w_up_b = w_up.astype(BF16)
    w_down_b = w_down.astype(BF16)

    g_mix3 = g_mix.reshape(depth, 1, d)
    g_cross3 = g_cross.reshape(depth, 1, d)
    g_mem3 = g_mem.reshape(depth, 1, d)
    g_ffn3 = g_ffn.reshape(depth, 1, d)
    subln_col = diff_subln.reshape(depth, HEAD_DIM, 1)
    conv_b3 = conv_b.reshape(depth, 1, D_FF)
    g_final2 = g_final.reshape(1, d)

    n_ab = N_HEADS_DIFF + N_HEADS_MOBA
    bias_ab = _causal_bias_tiles(rel_bias_table[:, :n_ab] * LOG2E, ATTN_TILE)
    bias_c = [_dilated_bias_tiles(rel_bias_table[:, n_ab:], window, dil) for window, dil in DIL_PAIRS]

    xf = x.reshape(t, d)
    memf = mem.reshape(b * n_mem, d)
    c_cols = 3 * N_HEADS_DIL

    for layer in range(depth):
        qkv = _norm_matmul(xf, g_mix3, w_in_b, layer, tm=1024, tn=512).reshape(b, s, 3 * MIX_WIDTH)
        o_a = _diff_attn(qkv, bias_ab, diff_lambda, subln_col, layer)
        o_b = _moba_attn(qkv, bias_ab)
        dil_outs = []
        for (window, dil), bias in zip(DIL_PAIRS, bias_c):
            if dil == 1:
                o, lse = _dil_attn(qkv, bias, 1, _QKV_COLS, _COL_QC)
            else:
                xq = qkv[:, :, _COL_QC * HEAD_DIM:].reshape(b, s // dil, dil * c_cols * HEAD_DIM)
                o, lse = _dil_attn(xq, bias, dil, c_cols, 0)
            dil_outs.append((o.reshape(t, W_DIL), lse.reshape(t, W_DIL)))
        xf = _out_proj(xf, o_a.reshape(t, W_DIFF), o_b.reshape(t, W_MOBA), dil_outs, w_out_b, layer,
                       tm=512, tn=1024)
        kv = _norm_matmul(memf, g_mem3, w_ckv_b, layer, tm=b * n_mem, tn=512)
        xf = _cross_attn(xf, g_cross3, w_cq_b, kv.reshape(b, n_mem, 2 * MEM_WIDTH), w_co_b, layer, s,
                         tm=512)
        xf = _conv_ffn(xf, g_ffn3, w_up_b, conv_w, conv_b3, w_down_b, g_final2, layer, s,
                       tm=512, tf=512, final_norm=(layer == depth - 1))
    return xf.reshape(b, s, d)
```

```python
import functools
import math

import numpy as np
import jax
import jax.numpy as jnp
from jax import lax
from jax.experimental import pallas as pl
from jax.experimental.pallas import tpu as pltpu

F32 = jnp.float32
BF16 = jnp.bfloat16

D_MODEL = 2048
HEAD_DIM = 128
N_HEADS_DIFF = 6
N_HEADS_MOBA = 4
N_HEADS_DIL = 6
W_DIFF = N_HEADS_DIFF * HEAD_DIM
W_MOBA = N_HEADS_MOBA * HEAD_DIM
W_DIL = N_HEADS_DIL * HEAD_DIM
MIX_WIDTH = W_DIFF + W_MOBA + W_DIL
DIFF_QK_DIM = HEAD_DIM // 2
MOBA_BLOCK = 256
MOBA_TOPK = 3
DIL_PAIRS = ((128, 1), (512, 4), (2048, 16))
DIL_BLOCK = 128
N_BUCKETS = 32
REL_MAX_DIST = 2048
N_MEM_HEADS = 4
MEM_WIDTH = N_MEM_HEADS * HEAD_DIM
D_FF = 5632
CONV_WIDTH = 3
NORM_EPS = 1e-6
NEG_INF = -1e30

_COL_QA = 0
_COL_KA = _COL_QA + N_HEADS_DIFF
_COL_VA = _COL_KA + N_HEADS_DIFF
_COL_QB = _COL_VA + N_HEADS_DIFF
_COL_KB = _COL_QB + N_HEADS_MOBA
_COL_VB = _COL_KB + N_HEADS_MOBA
_COL_QC = _COL_VB + N_HEADS_MOBA
_QKV_COLS = 3 * MIX_WIDTH // HEAD_DIM

ATTN_TILE = 256
Q_TILE = 512
KV_TILE = 1024
VT_ROWS = HEAD_DIM + 16
LOG2E = math.log2(math.e)
DIL_BATCH = 4
DIL_PAD = DIL_BLOCK * max(dil for _, dil in DIL_PAIRS)
FFN_HALO = 16
VMEM_LIMIT = 56 * 1024 * 1024

_NT = (((1,), (1,)), ((), ()))


def _params(semantics):
    return pltpu.CompilerParams(dimension_semantics=semantics, vmem_limit_bytes=VMEM_LIMIT)


def _rms_norm(xf, g):
    y = xf * lax.rsqrt(jnp.mean(xf * xf, axis=-1, keepdims=True) + NORM_EPS)
    return y * g


def _norm_matmul_kernel(x_ref, g_ref, w_ref, o_ref, hn_ref):
    @pl.when(pl.program_id(1) == 0)
    def _():
        hn_ref[...] = _rms_norm(x_ref[...], g_ref[...]).astype(BF16)

    o_ref[...] = jnp.dot(hn_ref[...], w_ref[...], preferred_element_type=F32).astype(o_ref.dtype)


def _norm_matmul(x, g, w, layer, *, tm, tn):
    t, d = x.shape
    n = w.shape[-1]
    return pl.pallas_call(
        _norm_matmul_kernel,
        grid=(t // tm, n // tn),
        in_specs=[
            pl.BlockSpec((tm, d), lambda i, j: (i, 0)),
            pl.BlockSpec((None, 1, d), lambda i, j: (layer, 0, 0)),
            pl.BlockSpec((None, d, tn), lambda i, j: (layer, 0, j)),
        ],
        out_specs=pl.BlockSpec((tm, tn), lambda i, j: (i, j)),
        out_shape=jax.ShapeDtypeStruct((t, n), BF16),
        scratch_shapes=[pltpu.VMEM((tm, d), BF16)],
        compiler_params=_params(("parallel", "arbitrary")),
        name="norm_matmul",
    )(x, g, w)


def _flash_step(s_t, vt_tile, m, acc):
    m_new = jnp.maximum(m, jnp.max(s_t, axis=0, keepdims=True))
    alpha = jnp.exp2(m - m_new)
    p = jnp.exp2(s_t - m_new)
    acc = alpha * acc + jnp.dot(vt_tile, p.astype(BF16), preferred_element_type=F32)
    return m_new, acc


def _bias_block(bias_ref, q_blk, k_blk, n_bias):
    cols = []
    for e in range(Q_TILE // ATTN_TILE):
        tiles = []
        for c in range(KV_TILE // ATTN_TILE):
            d = (q_blk + e) - (k_blk + c)
            tiles.append(bias_ref[0, jnp.where(d < 0, n_bias, jnp.minimum(d, n_bias - 1))])
        cols.append(jnp.concatenate(tiles, axis=0))
    return jnp.concatenate(cols, axis=1)


def _causal_sweep(i, logits_into, vt_ref, sa_ref, sb_ref, m_ref, acc_ref):
    tk = KV_TILE
    n_steps = ((i + 1) * Q_TILE - 1) // tk + 1

    def key_start(j):
        return pl.multiple_of(jnp.minimum(j, n_steps - 1) * tk, tk)

    def step(j, cur_ref, nxt_ref):
        logits_into(j + 1, key_start(j + 1), nxt_ref)
        m_new, acc = _flash_step(cur_ref[...], vt_ref[:, pl.ds(key_start(j), tk)], m_ref[...], acc_ref[...])
        m_ref[...] = m_new
        acc_ref[...] = acc

    m_ref[...] = jnp.full(m_ref.shape, NEG_INF, F32)
    acc_ref[...] = jnp.zeros(acc_ref.shape, F32)
    logits_into(0, key_start(0), sa_ref)

    def body(j, carry):
        @pl.when(j % 2 == 0)
        def _():
            step(j, sa_ref, sb_ref)

        @pl.when(j % 2 == 1)
        def _():
            step(j, sb_ref, sa_ref)
        return carry

    lax.fori_loop(0, n_steps, body, 0)
    return acc_ref[...]


def _sweep_scratch(seq_len, n):
    return [pltpu.VMEM((VT_ROWS, seq_len), BF16), pltpu.VMEM((KV_TILE, n), F32), pltpu.VMEM((KV_TILE, n), F32),
            pltpu.VMEM((1, n), F32), pltpu.VMEM((VT_ROWS, n), F32)]


def _transpose_values(v_ref, vt_ref, seq_len):
    def body(c, carry):
        start = pl.multiple_of(c * KV_TILE, KV_TILE)
        vt_ref[0:HEAD_DIM, pl.ds(start, KV_TILE)] = (
            v_ref[0, pl.ds(start, KV_TILE), :].astype(F32).T.astype(BF16))
        return carry
    lax.fori_loop(0, seq_len // KV_TILE, body, 0)
    vt_ref[HEAD_DIM:, :] = jnp.ones((VT_ROWS - HEAD_DIM, seq_len), BF16)


def _diff_attn_kernel(q_ref, k_ref, v_ref, bias_ref, lam_ref, sg_ref, o_ref, vt_ref, sa_ref, sb_ref,
                      m_ref, acc_ref, *, n_bias, lam_init, seq_len):
    tq, tk = Q_TILE, KV_TILE
    i = pl.program_id(2)

    @pl.when(i == 0)
    def _():
        _transpose_values(v_ref, vt_ref, seq_len)

    q_t = q_ref[0].astype(F32).T
    row = lax.broadcasted_iota(jnp.int32, q_t.shape, 0)
    qm_t = jnp.concatenate([jnp.where(row < DIFF_QK_DIM, q_t, 0.0),
                            jnp.where(row >= DIFF_QK_DIM, q_t, 0.0)], axis=1).astype(BF16)

    def logits_into(j, start, dst_ref):
        s_t = jnp.dot(k_ref[0, pl.ds(start, tk), :], qm_t, preferred_element_type=F32)
        b_t = _bias_block(bias_ref, i * (tq // ATTN_TILE), j * (tk // ATTN_TILE), n_bias)
        dst_ref[...] = s_t + jnp.concatenate([b_t, b_t], axis=1)

    acc = _causal_sweep(i, logits_into, vt_ref, sa_ref, sb_ref, m_ref, acc_ref)
    o_t = acc[:HEAD_DIM] / acc[HEAD_DIM:HEAD_DIM + 1]
    lp = lam_ref[...]
    lam = (jnp.exp(jnp.sum(lp[0:1] * lp[1:2], keepdims=True))
           - jnp.exp(jnp.sum(lp[2:3] * lp[3:4], keepdims=True)) + lam_init)
    od_t = o_t[:, :tq] - lam * o_t[:, tq:]
    y_t = od_t * lax.rsqrt(jnp.mean(od_t * od_t, axis=0, keepdims=True) + NORM_EPS) * sg_ref[...]
    o_ref[0] = (y_t * (1.0 - lam_init)).T.astype(o_ref.dtype)


def _diff_attn(qkv, bias_t, diff_lambda, diff_subln_col, layer):
    b, s, _ = qkv.shape
    n_bias = bias_t.shape[1] - 1
    assert s % KV_TILE == 0 and s % Q_TILE == 0
    lam_init = 0.8 - 0.6 * math.exp(-0.3 * layer)
    kern = functools.partial(_diff_attn_kernel, n_bias=n_bias, lam_init=lam_init, seq_len=s)
    return pl.pallas_call(
        kern,
        grid=(b, N_HEADS_DIFF, s // Q_TILE),
        in_specs=[
            pl.BlockSpec((1, Q_TILE, HEAD_DIM), lambda bi, h, i: (bi, i, _COL_QA + h)),
            pl.BlockSpec((1, s, HEAD_DIM), lambda bi, h, i: (bi, 0, _COL_KA + h)),
            pl.BlockSpec((1, s, HEAD_DIM), lambda bi, h, i: (bi, 0, _COL_VA + h)),
            pl.BlockSpec((1, n_bias + 1, ATTN_TILE, ATTN_TILE), lambda bi, h, i: (h, 0, 0, 0)),
            pl.BlockSpec((None, 4, DIFF_QK_DIM), lambda bi, h, i: (layer, 0, 0)),
            pl.BlockSpec((None, HEAD_DIM, 1), lambda bi, h, i: (layer, 0, 0)),
        ],
        out_specs=pl.BlockSpec((1, Q_TILE, HEAD_DIM), lambda bi, h, i: (bi, i, h)),
        out_shape=jax.ShapeDtypeStruct((b, s, W_DIFF), BF16),
        scratch_shapes=_sweep_scratch(s, 2 * Q_TILE),
        compiler_params=_params(("parallel", "parallel", "arbitrary")),
        name="diff_attn",
    )(qkv, qkv, qkv, bias_t, diff_lambda, diff_subln_col)


def _moba_kernel(q_ref, k_ref, v_ref, lanes_ref, bias_ref, o_ref, vt_ref, sa_ref, sb_ref, m_ref, acc_ref,
                 kmean_ref, *, n_bias, n_blocks, seq_len):
    tq, tk = Q_TILE, KV_TILE
    i = pl.program_id(2)

    @pl.when(i == 0)
    def _():
        _transpose_values(v_ref, vt_ref, seq_len)
        kf = k_ref[0].astype(F32).reshape(n_blocks, MOBA_BLOCK, HEAD_DIM)
        kmean_ref[...] = jnp.mean(kf, axis=1)

    q_t = q_ref[0].astype(F32).T
    gate = jnp.dot(kmean_ref[...], q_t, preferred_element_type=F32,
                   precision=lax.Precision.HIGHEST)
    blk = lax.broadcasted_iota(jnp.int32, gate.shape, 0)
    lane = lax.broadcasted_iota(jnp.int32, gate.shape, 1)
    own = i * (tq // MOBA_BLOCK)
    for e in range(1, tq // MOBA_BLOCK):
        own = own + (lane >= e * MOBA_BLOCK).astype(jnp.int32)
    past = blk < own
    gm = jnp.where(past, gate, NEG_INF)
    rank = jnp.zeros(gate.shape, jnp.int32)
    for jp in range(n_blocks):
        gj = gm[jp:jp + 1, :]
        beats = (gj > gm) | ((gj == gm) & (jp < blk))
        rank = rank + beats.astype(jnp.int32)
    allowed = (past & (rank < MOBA_TOPK)) | (blk == own)
    mask_t = jnp.where(allowed, 0.0, NEG_INF)
    mask_t = jnp.concatenate([mask_t, jnp.zeros((HEAD_DIM - n_blocks, tq), F32)], axis=0)
    q_aug_t = jnp.concatenate([q_t.astype(BF16), mask_t.astype(BF16)], axis=0)

    def logits_into(j, start, dst_ref):
        k_aug = jnp.concatenate([k_ref[0, pl.ds(start, tk), :], lanes_ref[pl.ds(start, tk), :]], axis=1)
        s_t = jnp.dot(k_aug, q_aug_t, preferred_element_type=F32)
        dst_ref[...] = s_t + _bias_block(bias_ref, i * (tq // ATTN_TILE), j * (tk // ATTN_TILE), n_bias)

    acc = _causal_sweep(i, logits_into, vt_ref, sa_ref, sb_ref, m_ref, acc_ref)
    o_ref[0] = (acc[:HEAD_DIM] / acc[HEAD_DIM:HEAD_DIM + 1]).T.astype(o_ref.dtype)


def _moba_attn(qkv, bias_t):
    b, s, _ = qkv.shape
    n_bias = bias_t.shape[1] - 1
    n_blocks = s // MOBA_BLOCK
    assert s % KV_TILE == 0 and s % Q_TILE == 0 and n_blocks <= HEAD_DIM and n_blocks % 8 == 0
    block_lanes = np.zeros((s, HEAD_DIM), np.float32)
    block_lanes[np.arange(s), np.arange(s) // MOBA_BLOCK] = 1.0
    kern = functools.partial(_moba_kernel, n_bias=n_bias, n_blocks=n_blocks, seq_len=s)
    return pl.pallas_call(
        kern,
        grid=(b, N_HEADS_MOBA, s // Q_TILE),
        in_specs=[
            pl.BlockSpec((1, Q_TILE, HEAD_DIM), lambda bi, h, i: (bi, i, _COL_QB + h)),
            pl.BlockSpec((1, s, HEAD_DIM), lambda bi, h, i: (bi, 0, _COL_KB + h)),
            pl.BlockSpec((1, s, HEAD_DIM), lambda bi, h, i: (bi, 0, _COL_VB + h)),
            pl.BlockSpec((s, HEAD_DIM), lambda bi, h, i: (0, 0)),
            pl.BlockSpec((1, n_bias + 1, ATTN_TILE, ATTN_TILE), lambda bi, h, i: (N_HEADS_DIFF + h, 0, 0, 0)),
        ],
        out_specs=pl.BlockSpec((1, Q_TILE, HEAD_DIM), lambda bi, h, i: (bi, i, h)),
        out_shape=jax.ShapeDtypeStruct((b, s, W_MOBA), BF16),
        scratch_shapes=_sweep_scratch(s, Q_TILE) + [pltpu.VMEM((n_blocks, HEAD_DIM), F32)],
        compiler_params=_params(("parallel", "parallel", "arbitrary")),
        name="moba_attn",
    )(qkv, qkv, qkv, jnp.asarray(block_lanes, BF16), bias_t)


def _dil_kernel(q_ref, k_ref, v_ref, bias_ref, o_ref, qf_ref, kf_ref, vf_ref, m_ref, l_ref, acc_ref,
                *, seq_len):
    blk, nb = DIL_BLOCK, DIL_BATCH
    rows = nb * blk
    chunk = KV_TILE

    kf_ref[0:DIL_PAD, :] = jnp.zeros((DIL_PAD, HEAD_DIM), F32)
    vf_ref[0:DIL_PAD, :] = jnp.zeros((DIL_PAD, HEAD_DIM), F32)

    def widen(c, carry):
        src = pl.ds(pl.multiple_of(c * chunk, chunk), chunk)
        dst = pl.ds(pl.multiple_of(DIL_PAD + c * chunk, chunk), chunk)
        qf_ref[src, :] = q_ref[0, src, :].astype(F32)
        kf_ref[dst, :] = k_ref[0, src, :].astype(F32)
        vf_ref[dst, :] = v_ref[0, src, :].astype(F32)
        return carry
    lax.fori_loop(0, seq_len // chunk, widen, 0)

    for br, (_, dil) in enumerate(DIL_PAIRS):
        trips_per_res = seq_len // dil // rows

        def rows_of(start, n, dil=dil):
            return pl.ds(start, n, stride=dil) if dil > 1 else pl.ds(start, n)

        def trip(idx, carry, br=br, dil=dil, trips_per_res=trips_per_res, rows_of=rows_of):
            res = idx // trips_per_res
            t = idx % trips_per_res
            tok = res + t * rows * dil
            q = qf_ref[rows_of(tok, rows), :].astype(BF16).reshape(nb, blk, HEAD_DIM)
            kk = kf_ref[rows_of(DIL_PAD + tok - blk * dil, rows + blk), :].astype(BF16)
            vv = vf_ref[rows_of(DIL_PAD + tok - blk * dil, rows + blk), :].astype(BF16)
            k_prev, k_cur = kk[:rows].reshape(nb, blk, HEAD_DIM), kk[blk:].reshape(nb, blk, HEAD_DIM)
            v_prev, v_cur = vv[:rows].reshape(nb, blk, HEAD_DIM), vv[blk:].reshape(nb, blk, HEAD_DIM)
            b_first = bias_ref[br, 0, jnp.where(t == 0, 2, 0)]
            b_prev = jnp.concatenate(
                [b_first[None], jnp.broadcast_to(bias_ref[br, 0, 0][None], (nb - 1, blk, blk))], axis=0)
            s_prev = jnp.einsum("nqd,nkd->nqk", q, k_prev, preferred_element_type=F32) + b_prev
            s_cur = jnp.einsum("nqd,nkd->nqk", q, k_cur, preferred_element_type=F32) + bias_ref[br, 0, 1][None]
            m_b = jnp.maximum(jnp.max(s_prev, axis=-1, keepdims=True), jnp.max(s_cur, axis=-1, keepdims=True))
            e_prev = jnp.exp(s_prev - m_b)
            e_cur = jnp.exp(s_cur - m_b)
            l_b = jnp.sum(e_prev, axis=-1, keepdims=True) + jnp.sum(e_cur, axis=-1, keepdims=True)
            o_b = (jnp.einsum("nqk,nkd->nqd", e_prev.astype(BF16), v_prev, preferred_element_type=F32)
                   + jnp.einsum("nqk,nkd->nqd", e_cur.astype(BF16), v_cur, preferred_element_type=F32))
            m_b = jnp.broadcast_to(m_b.reshape(rows, 1), (rows, HEAD_DIM))
            l_b = jnp.broadcast_to(l_b.reshape(rows, 1), (rows, HEAD_DIM))
            o_b = o_b.reshape(rows, HEAD_DIM)
            dst = rows_of(tok, rows)
            if br == 0:
                m_ref[dst, :], l_ref[dst, :], acc_ref[dst, :] = m_b, l_b, o_b
            else:
                m_old = m_ref[dst, :]
                m_new = jnp.maximum(m_old, m_b)
                w_old = jnp.exp(m_old - m_new)
                w_b = jnp.exp(m_b - m_new)
                m_ref[dst, :] = m_new
                l_ref[dst, :] = w_old * l_ref[dst, :] + w_b * l_b
                acc_ref[dst, :] = w_old * acc_ref[dst, :] + w_b * o_b
            return carry
        lax.fori_loop(0, dil * trips_per_res, trip, 0)

    def finish(c, carry):
        src = pl.ds(pl.multiple_of(c * chunk, chunk), chunk)
        o_ref[0, src, :] = (acc_ref[src, :] / l_ref[src, :]).astype(o_ref.dtype)
        return carry
    lax.fori_loop(0, seq_len // chunk, finish, 0)


def _dil_attn(qkv, bias):
    b, s, _ = qkv.shape
    n_br = len(DIL_PAIRS)
    assert all(s % (dil * DIL_BATCH * DIL_BLOCK) == 0 for _, dil in DIL_PAIRS) and s % KV_TILE == 0
    slab = lambda col: pl.BlockSpec((1, s, HEAD_DIM), lambda bi, h: (bi, 0, col + h))
    return pl.pallas_call(
        functools.partial(_dil_kernel, seq_len=s),
        grid=(b, N_HEADS_DIL),
        in_specs=[slab(_COL_QC), slab(_COL_QC + N_HEADS_DIL), slab(_COL_QC + 2 * N_HEADS_DIL),
                  pl.BlockSpec((n_br, 1, 3, DIL_BLOCK, DIL_BLOCK), lambda bi, h: (0, h, 0, 0, 0))],
        out_specs=pl.BlockSpec((1, s, HEAD_DIM), lambda bi, h: (bi, 0, h)),
        out_shape=jax.ShapeDtypeStruct((b, s, W_DIL), BF16),
        scratch_shapes=[pltpu.VMEM((s, HEAD_DIM), F32), pltpu.VMEM((DIL_PAD + s, HEAD_DIM), F32),
                        pltpu.VMEM((DIL_PAD + s, HEAD_DIM), F32), pltpu.VMEM((s, HEAD_DIM), F32),
                        pltpu.VMEM((s, HEAD_DIM), F32), pltpu.VMEM((s, HEAD_DIM), F32)],
        compiler_params=_params(("parallel", "parallel")),
        name="dil_attn",
    )(qkv, qkv, qkv, bias)


def _out_proj_kernel(x_ref, a_ref, b_ref, c_ref, w_ref, o_ref):
    acc = jnp.dot(a_ref[...], w_ref[0:W_DIFF, :], preferred_element_type=F32)
    acc += jnp.dot(b_ref[...], w_ref[W_DIFF:W_DIFF + W_MOBA, :], preferred_element_type=F32)
    acc += jnp.dot(c_ref[...], w_ref[W_DIFF + W_MOBA:, :], preferred_element_type=F32)
    o_ref[...] = x_ref[...] + acc


def _out_proj(x, o_a, o_b, o_c, w_out, layer, *, tm, tn):
    t, d = x.shape
    row = lambda width: pl.BlockSpec((tm, width), lambda i, j: (i, 0))
    return pl.pallas_call(
        _out_proj_kernel,
        grid=(t // tm, d // tn),
        in_specs=[pl.BlockSpec((tm, tn), lambda i, j: (i, j)), row(W_DIFF), row(W_MOBA), row(W_DIL),
                  pl.BlockSpec((None, MIX_WIDTH, tn), lambda i, j: (layer, 0, j))],
        out_specs=pl.BlockSpec((tm, tn), lambda i, j: (i, j)),
        out_shape=jax.ShapeDtypeStruct((t, d), F32),
        compiler_params=_params(("parallel", "parallel")),
        name="out_proj",
    )(x, o_a, o_b, o_c, w_out)


def _cross_attn_kernel(x_ref, g_ref, wq_ref, kv_ref, wo_ref, o_ref):
    x = x_ref[...]
    hn = _rms_norm(x, g_ref[...]).astype(BF16)
    q = jnp.dot(hn, wq_ref[...], preferred_element_type=F32).astype(BF16)
    kv = kv_ref[0]
    heads = []
    for h in range(N_MEM_HEADS):
        lo = h * HEAD_DIM
        s = lax.dot_general(q[:, lo:lo + HEAD_DIM], kv[:, lo:lo + HEAD_DIM], _NT,
                            preferred_element_type=F32)
        e = jnp.exp(s - jnp.max(s, axis=-1, keepdims=True))
        oh = jnp.dot(e.astype(BF16), kv[:, MEM_WIDTH + lo:MEM_WIDTH + lo + HEAD_DIM],
                     preferred_element_type=F32)
        heads.append((oh / jnp.sum(e, axis=-1, keepdims=True)).astype(BF16))
    o = jnp.concatenate(heads, axis=1)
    o_ref[...] = x + jnp.dot(o, wo_ref[...], preferred_element_type=F32)


def _cross_attn(x, g, w_cq, kv, w_co, layer, seq_len, *, tm):
    t, d = x.shape
    n_mem = kv.shape[1]
    tiles_per_batch = seq_len // tm
    return pl.pallas_call(
        _cross_attn_kernel,
        grid=(t // tm,),
        in_specs=[
            pl.BlockSpec((tm, d), lambda i: (i, 0)),
            pl.BlockSpec((None, 1, d), lambda i: (layer, 0, 0)),
            pl.BlockSpec((None, d, MEM_WIDTH), lambda i: (layer, 0, 0)),
            pl.BlockSpec((1, n_mem, 2 * MEM_WIDTH), lambda i: (i // tiles_per_batch, 0, 0)),
            pl.BlockSpec((None, MEM_WIDTH, d), lambda i: (layer, 0, 0)),
        ],
        out_specs=pl.BlockSpec((tm, d), lambda i: (i, 0)),
        out_shape=jax.ShapeDtypeStruct((t, d), F32),
        compiler_params=_params(("parallel",)),
        name="cross_attn",
    )(x, g, w_cq, kv, w_co)


def _ffn_kernel(x_ref, xp_ref, g_ref, wg_ref, wu_ref, cw_ref, cb_ref, wd_ref, gf_ref, o_ref, hn_ref,
                *, tm, tiles_per_batch, final_norm):
    i = pl.program_id(0)
    f = pl.program_id(1)

    @pl.when(f == 0)
    def _():
        g = g_ref[...]
        hn_ref[FFN_HALO:, :] = _rms_norm(x_ref[...], g).astype(BF16)
        prev = _rms_norm(xp_ref[...], g)
        prev = jnp.where(i % tiles_per_batch == 0, 0.0, prev)
        hn_ref[0:FFN_HALO, :] = prev.astype(BF16)
        o_ref[...] = x_ref[...]

    gate = jnp.dot(hn_ref[...], wg_ref[...], preferred_element_type=F32)
    up = jnp.dot(hn_ref[FFN_HALO:, :], wu_ref[...], preferred_element_type=F32)
    cw = cw_ref[...]
    conv = (cw[0:1] * gate[FFN_HALO - 2:FFN_HALO - 2 + tm]
            + cw[1:2] * gate[FFN_HALO - 1:FFN_HALO - 1 + tm]
            + cw[2:3] * gate[FFN_HALO:]) + cb_ref[...]
    act = conv * jax.nn.sigmoid(conv) * up
    o_ref[...] += jnp.dot(act.astype(BF16), wd_ref[...], preferred_element_type=F32)

    if final_norm:
        @pl.when(f == pl.num_programs(1) - 1)
        def _():
            o_ref[...] = _rms_norm(o_ref[...], gf_ref[...])


def _conv_ffn(x, g, w_up, conv_w, conv_b, w_down, g_final, layer, seq_len, *, tm, tf, final_norm):
    t, d = x.shape
    n_f = D_FF // tf
    tiles_per_batch = seq_len // tm
    halo_blocks = tm // FFN_HALO
    kern = functools.partial(_ffn_kernel, tm=tm, tiles_per_batch=tiles_per_batch, final_norm=final_norm)
    return pl.pallas_call(
        kern,
        grid=(t // tm, n_f),
        in_specs=[
            pl.BlockSpec((tm, d), lambda i, f: (i, 0)),
            pl.BlockSpec((FFN_HALO, d), lambda i, f: (jnp.maximum(i * halo_blocks - 1, 0), 0)),
            pl.BlockSpec((None, 1, d), lambda i, f: (layer, 0, 0)),
            pl.BlockSpec((None, d, tf), lambda i, f: (layer, 0, f)),
            pl.BlockSpec((None, d, tf), lambda i, f: (layer, 0, n_f + f)),
            pl.BlockSpec((None, CONV_WIDTH, tf), lambda i, f: (layer, 0, f)),
            pl.BlockSpec((None, 1, tf), lambda i, f: (layer, 0, f)),
            pl.BlockSpec((None, tf, d), lambda i, f: (layer, f, 0)),
            pl.BlockSpec((1, d), lambda i, f: (0, 0)),
        ],
        out_specs=pl.BlockSpec((tm, d), lambda i, f: (i, 0)),
        out_shape=jax.ShapeDtypeStruct((t, d), F32),
        scratch_shapes=[pltpu.VMEM((FFN_HALO + tm, d), BF16)],
        compiler_params=_params(("parallel", "arbitrary")),
        name="conv_ffn",
    )(x, x, g, w_up, w_up, conv_w, conv_b, w_down, g_final)


def _t5_bucket(dist):
    n = jnp.maximum(dist, 0)
    max_exact = N_BUCKETS // 2
    nf = jnp.maximum(n, 1).astype(F32)
    log_ratio = jnp.log(nf / max_exact) / math.log(REL_MAX_DIST / max_exact)
    large = max_exact + (log_ratio * (N_BUCKETS - max_exact)).astype(jnp.int32)
    large = jnp.minimum(large, N_BUCKETS - 1)
    return jnp.where(n < max_exact, n, large)


def _toeplitz(w, rows, cols, offset):
    length = cols + offset + 1
    assert w.shape[-1] == length and offset >= rows - 1
    lead = w.shape[:-1]
    flat = jnp.broadcast_to(w[..., None, :], lead + (rows, length)).reshape(lead + (rows * length,))
    shifted = flat[..., :rows * (length - 1)].reshape(lead + (rows, length - 1))
    return shifted[..., offset:offset + cols]


def _causal_bias_tiles(table, tile):
    n_tiles = -(-(REL_MAX_DIST + tile - 1) // tile) + 1
    dist = np.arange(-(tile - 1), n_tiles * tile + 1, dtype=np.int32)
    vec = jnp.where(dist[None, :] >= 0, table.astype(F32)[_t5_bucket(jnp.asarray(dist))].T, NEG_INF)
    tiles = _toeplitz(vec, tile, n_tiles * tile, tile - 1)
    tiles = tiles.reshape(-1, tile, n_tiles, tile).transpose(0, 2, 1, 3)
    return jnp.concatenate([tiles, jnp.full((tiles.shape[0], 1, tile, tile), NEG_INF, F32)], axis=1)


def _dilated_bias_tiles(table, window, dil):
    blk = DIL_BLOCK
    steps = window // dil
    j = np.arange(-(blk - 1), 2 * blk + 1, dtype=np.int32)
    valid = (j >= 0) & (j <= steps)
    bias = table.astype(F32)[_t5_bucket(jnp.asarray(np.maximum(j, 0) * dil))].T
    vec = jnp.where(valid[None, :], bias, NEG_INF)
    tiles = _toeplitz(vec, blk, 2 * blk, blk - 1)
    tiles = tiles.reshape(-1, blk, 2, blk).transpose(0, 2, 3, 1)
    masked = jnp.full((tiles.shape[0], 1, blk, blk), NEG_INF, F32)
    return jnp.concatenate([tiles[:, 1:2], tiles[:, 0:1], masked], axis=1)


def kernel(x, mem, w_in, w_out, g_mix, diff_lambda, diff_subln, rel_bias_table, g_cross, g_mem,
           w_cq, w_ckv, w_co, g_ffn, w_up, conv_w, conv_b, w_down, g_final):
    b, s, d = x.shape
    depth = w_in.shape[0]
    n_mem = mem.shape[1]
    t = b * s

    col_scale = np.ones((3 * MIX_WIDTH,), np.float32)
    col_scale[_COL_QA * HEAD_DIM:_COL_KA * HEAD_DIM] = DIFF_QK_DIM ** -0.5 * LOG2E
    col_scale[_COL_QB * HEAD_DIM:_COL_KB * HEAD_DIM] = HEAD_DIM ** -0.5 * LOG2E
    col_scale[_COL_QC * HEAD_DIM:(_COL_QC + N_HEADS_DIL) * HEAD_DIM] = HEAD_DIM ** -0.5
    w_in_b = (w_in * col_scale).astype(BF16)
    w_out_b = w_out.astype(BF16)
    w_cq_b = (w_cq * HEAD_DIM ** -0.5).astype(BF16)
    w_ckv_b = w_ckv.astype(BF16)
    w_co_b = w_co.astype(BF16)
    w_up_b = w_up.astype(BF16)
    w_down_b = w_down.astype(BF16)

    g_mix3 = g_mix.reshape(depth, 1, d)
    g_cross3 = g_cross.reshape(depth, 1, d)
    g_mem3 = g_mem.reshape(depth, 1, d)
    g_ffn3 = g_ffn.reshape(depth, 1, d)
    subln_col = diff_subln.reshape(depth, HEAD_DIM, 1)
    conv_b3 = conv_b.reshape(depth, 1, D_FF)
    g_final2 = g_final.reshape(1, d)

    n_ab = N_HEADS_DIFF + N_HEADS_MOBA
    bias_ab = _causal_bias_tiles(rel_bias_table[:, :n_ab] * LOG2E, ATTN_TILE)
    bias_c = jnp.stack([_dilated_bias_tiles(rel_bias_table[:, n_ab:], window, dil)
                        for window, dil in DIL_PAIRS])

    xf = x.reshape(t, d)
    memf = mem.reshape(b * n_mem, d)

    for layer in range(depth):
        qkv = _norm_matmul(xf, g_mix3, w_in_b, layer, tm=1024, tn=512).reshape(b, s, 3 * MIX_WIDTH)
        o_a = _diff_attn(qkv, bias_ab, diff_lambda, subln_col, layer)
        o_b = _moba_attn(qkv, bias_ab)
        o_c = _dil_attn(qkv, bias_c)
        xf = _out_proj(xf, o_a.reshape(t, W_DIFF), o_b.reshape(t, W_MOBA), o_c.reshape(t, W_DIL),
                       w_out_b, layer, tm=1024, tn=1024)
        kv = _norm_matmul(memf, g_mem3, w_ckv_b, layer, tm=b * n_mem, tn=512)
        xf = _cross_attn(xf, g_cross3, w_cq_b, kv.reshape(b, n_mem, 2 * MEM_WIDTH), w_co_b, layer, s,
                         tm=512)
        xf = _conv_ffn(xf, g_ffn3, w_up_b, conv_w, conv_b3, w_down_b, g_final2, layer, s,
                       tm=512, tf=512, final_norm=(layer == depth - 1))
    return xf.reshape(b, s, d)
```

```python
import functools
import math

import numpy as np
import jax
import jax.numpy as jnp
from jax import lax
from jax.experimental import pallas as pl
from jax.experimental.pallas import tpu as pltpu

F32 = jnp.float32
BF16 = jnp.bfloat16

D_MODEL = 2048
HEAD_DIM = 128
N_HEADS_DIFF = 6
N_HEADS_MOBA = 4
N_HEADS_DIL = 6
W_DIFF = N_HEADS_DIFF * HEAD_DIM
W_MOBA = N_HEADS_MOBA * HEAD_DIM
W_DIL = N_HEADS_DIL * HEAD_DIM
MIX_WIDTH = W_DIFF + W_MOBA + W_DIL
DIFF_QK_DIM = HEAD_DIM // 2
MOBA_BLOCK = 256
MOBA_TOPK = 3
DIL_PAIRS = ((128, 1), (512, 4), (2048, 16))
DIL_BLOCK = 128
N_BUCKETS = 32
REL_MAX_DIST = 2048
N_MEM_HEADS = 4
MEM_WIDTH = N_MEM_HEADS * HEAD_DIM
D_FF = 5632
CONV_WIDTH = 3
NORM_EPS = 1e-6
NEG_INF = -1e30

_COL_QA = 0
_COL_KA = _COL_QA + N_HEADS_DIFF
_COL_VA = _COL_KA + N_HEADS_DIFF
_COL_QB = _COL_VA + N_HEADS_DIFF
_COL_KB = _COL_QB + N_HEADS_MOBA
_COL_VB = _COL_KB + N_HEADS_MOBA
_COL_QC = _COL_VB + N_HEADS_MOBA
_QKV_COLS = 3 * MIX_WIDTH // HEAD_DIM

ATTN_TILE = 256
Q_TILE = 1024
KV_TILE = 1024
VT_ROWS = HEAD_DIM + 16
LOG2E = math.log2(math.e)
DIL_BATCH = 4
DIL_PAD = DIL_BLOCK * max(dil for _, dil in DIL_PAIRS)
FFN_HALO = 16
VMEM_LIMIT = 56 * 1024 * 1024

_NT = (((1,), (1,)), ((), ()))


def _params(semantics):
    return pltpu.CompilerParams(dimension_semantics=semantics, vmem_limit_bytes=VMEM_LIMIT)


def _rms_norm(xf, g):
    y = xf * lax.rsqrt(jnp.mean(xf * xf, axis=-1, keepdims=True) + NORM_EPS)
    return y * g


def _norm_matmul_kernel(x_ref, g_ref, w_ref, o_ref, hn_ref):
    @pl.when(pl.program_id(1) == 0)
    def _():
        hn_ref[...] = _rms_norm(x_ref[...], g_ref[...]).astype(BF16)

    o_ref[...] = jnp.dot(hn_ref[...], w_ref[...], preferred_element_type=F32).astype(o_ref.dtype)


def _norm_matmul(x, g, w, layer, *, tm, tn):
    t, d = x.shape
    n = w.shape[-1]
    return pl.pallas_call(
        _norm_matmul_kernel,
        grid=(t // tm, n // tn),
        in_specs=[
            pl.BlockSpec((tm, d), lambda i, j: (i, 0)),
            pl.BlockSpec((None, 1, d), lambda i, j: (layer, 0, 0)),
            pl.BlockSpec((None, d, tn), lambda i, j: (layer, 0, j)),
        ],
        out_specs=pl.BlockSpec((tm, tn), lambda i, j: (i, j)),
        out_shape=jax.ShapeDtypeStruct((t, n), BF16),
        scratch_shapes=[pltpu.VMEM((tm, d), BF16)],
        compiler_params=_params(("parallel", "arbitrary")),
        name="norm_matmul",
    )(x, g, w)


def _bias_block(bias_ref, q_blk, k_blk, n_bias):
    cols = []
    for e in range(Q_TILE // ATTN_TILE):
        tiles = []
        for c in range(KV_TILE // ATTN_TILE):
            d = (q_blk + e) - (k_blk + c)
            tiles.append(bias_ref[0, jnp.where(d < 0, n_bias, jnp.minimum(d, n_bias - 1))])
        cols.append(jnp.concatenate(tiles, axis=0))
    return jnp.concatenate(cols, axis=1)


def _causal_sweep(i, logits_into, vt_ref, s_refs, m_ref, acc_ref):
    tk = KV_TILE
    n_steps = ((i + 1) * Q_TILE - 1) // tk + 1

    def key_start(j):
        return pl.multiple_of(jnp.minimum(j, n_steps - 1) * tk, tk)

    def step(j, slot):
        logits_into(j + 1, key_start(j + 1), s_refs[1 - slot])
        s_t = s_refs[slot][...]
        m = m_ref[...]
        m_new = jnp.maximum(m, jnp.max(s_t, axis=0, keepdims=True))
        p = jnp.exp2(s_t - m_new).astype(BF16)
        pv = jnp.dot(vt_ref[:, pl.ds(key_start(j), tk)], p, preferred_element_type=F32)
        acc_ref[...] = jnp.exp2(m - m_new) * acc_ref[...] + pv
        m_ref[...] = m_new

    m_ref[...] = jnp.full(m_ref.shape, NEG_INF, F32)
    acc_ref[...] = jnp.zeros(acc_ref.shape, F32)
    logits_into(0, key_start(0), s_refs[0])

    def body(j, carry):
        for slot in range(2):
            @pl.when(j % 2 == slot)
            def _(slot=slot):
                step(j, slot)
        return carry

    lax.fori_loop(0, n_steps, body, 0)
    return acc_ref[...]


def _sweep_scratch(seq_len, n):
    return [pltpu.VMEM((VT_ROWS, seq_len), BF16), pltpu.VMEM((KV_TILE, n), F32), pltpu.VMEM((KV_TILE, n), F32),
            pltpu.VMEM((1, n), F32), pltpu.VMEM((VT_ROWS, n), F32)]


def _transpose_values(v_ref, vt_ref, seq_len):
    def body(c, carry):
        start = pl.multiple_of(c * KV_TILE, KV_TILE)
        vt_ref[0:HEAD_DIM, pl.ds(start, KV_TILE)] = (
            v_ref[0, pl.ds(start, KV_TILE), :].astype(F32).T.astype(BF16))
        return carry
    lax.fori_loop(0, seq_len // KV_TILE, body, 0)
    vt_ref[HEAD_DIM:, :] = jnp.ones((VT_ROWS - HEAD_DIM, seq_len), BF16)


def _diff_attn_kernel(q_ref, k_ref, v_ref, bias_ref, lam_ref, sg_ref, o_ref, vt_ref, sa_ref, sb_ref,
                      m_ref, acc_ref, *, n_bias, lam_init, seq_len):
    tq, tk = Q_TILE, KV_TILE
    i = pl.program_id(2)

    @pl.when(i == 0)
    def _():
        _transpose_values(v_ref, vt_ref, seq_len)

    q_t = q_ref[0].astype(F32).T
    row = lax.broadcasted_iota(jnp.int32, q_t.shape, 0)
    qm_t = jnp.concatenate([jnp.where(row < DIFF_QK_DIM, q_t, 0.0),
                            jnp.where(row >= DIFF_QK_DIM, q_t, 0.0)], axis=1).astype(BF16)

    def logits_into(j, start, dst_ref):
        s_t = jnp.dot(k_ref[0, pl.ds(start, tk), :], qm_t, preferred_element_type=F32)
        b_t = _bias_block(bias_ref, i * (tq // ATTN_TILE), j * (tk // ATTN_TILE), n_bias)
        dst_ref[...] = s_t + jnp.concatenate([b_t, b_t], axis=1)

    acc = _causal_sweep(i, logits_into, vt_ref, (sa_ref, sb_ref), m_ref, acc_ref)
    o_t = acc[:HEAD_DIM] / acc[HEAD_DIM:HEAD_DIM + 1]
    lp = lam_ref[...]
    lam = (jnp.exp(jnp.sum(lp[0:1] * lp[1:2], keepdims=True))
           - jnp.exp(jnp.sum(lp[2:3] * lp[3:4], keepdims=True)) + lam_init)
    od_t = o_t[:, :tq] - lam * o_t[:, tq:]
    y_t = od_t * lax.rsqrt(jnp.mean(od_t * od_t, axis=0, keepdims=True) + NORM_EPS) * sg_ref[...]
    o_ref[0] = (y_t * (1.0 - lam_init)).T.astype(o_ref.dtype)


def _diff_attn(qkv, bias_t, diff_lambda, diff_subln_col, layer):
    b, s, _ = qkv.shape
    n_bias = bias_t.shape[1] - 1
    assert s % KV_TILE == 0 and s % Q_TILE == 0
    lam_init = 0.8 - 0.6 * math.exp(-0.3 * layer)
    kern = functools.partial(_diff_attn_kernel, n_bias=n_bias, lam_init=lam_init, seq_len=s)
    return pl.pallas_call(
        kern,
        grid=(b, N_HEADS_DIFF, s // Q_TILE),
        in_specs=[
            pl.BlockSpec((1, Q_TILE, HEAD_DIM), lambda bi, h, i: (bi, i, _COL_QA + h)),
            pl.BlockSpec((1, s, HEAD_DIM), lambda bi, h, i: (bi, 0, _COL_KA + h)),
            pl.BlockSpec((1, s, HEAD_DIM), lambda bi, h, i: (bi, 0, _COL_VA + h)),
            pl.BlockSpec((1, n_bias + 1, ATTN_TILE, ATTN_TILE), lambda bi, h, i: (h, 0, 0, 0)),
            pl.BlockSpec((None, 4, DIFF_QK_DIM), lambda bi, h, i: (layer, 0, 0)),
            pl.BlockSpec((None, HEAD_DIM, 1), lambda bi, h, i: (layer, 0, 0)),
        ],
        out_specs=pl.BlockSpec((1, Q_TILE, HEAD_DIM), lambda bi, h, i: (bi, i, h)),
        out_shape=jax.ShapeDtypeStruct((b, s, W_DIFF), BF16),
        scratch_shapes=_sweep_scratch(s, 2 * Q_TILE),
        compiler_params=_params(("parallel", "parallel", "arbitrary")),
        name="diff_attn",
    )(qkv, qkv, qkv, bias_t, diff_lambda, diff_subln_col)


def _moba_kernel(q_ref, k_ref, v_ref, lanes_ref, bias_ref, o_ref, vt_ref, sa_ref, sb_ref, m_ref, acc_ref,
                 kmean_ref, *, n_bias, n_blocks, seq_len):
    tq, tk = Q_TILE, KV_TILE
    i = pl.program_id(2)

    @pl.when(i == 0)
    def _():
        _transpose_values(v_ref, vt_ref, seq_len)
        kf = k_ref[0].astype(F32).reshape(n_blocks, MOBA_BLOCK, HEAD_DIM)
        kmean_ref[...] = jnp.mean(kf, axis=1)

    q_t = q_ref[0].astype(F32).T
    gate = jnp.dot(kmean_ref[...], q_t, preferred_element_type=F32,
                   precision=lax.Precision.HIGHEST)
    blk = lax.broadcasted_iota(jnp.int32, gate.shape, 0)
    lane = lax.broadcasted_iota(jnp.int32, gate.shape, 1)
    own = i * (tq // MOBA_BLOCK)
    for e in range(1, tq // MOBA_BLOCK):
        own = own + (lane >= e * MOBA_BLOCK).astype(jnp.int32)
    past = blk < own
    gm = jnp.where(past, gate, NEG_INF)
    rank = jnp.zeros(gate.shape, jnp.int32)
    for jp in range(n_blocks):
        gj = gm[jp:jp + 1, :]
        beats = (gj > gm) | ((gj == gm) & (jp < blk))
        rank = rank + beats.astype(jnp.int32)
    allowed = (past & (rank < MOBA_TOPK)) | (blk == own)
    mask_t = jnp.where(allowed, 0.0, NEG_INF)
    mask_t = jnp.concatenate([mask_t, jnp.zeros((HEAD_DIM - n_blocks, tq), F32)], axis=0)
    q_aug_t = jnp.concatenate([q_t.astype(BF16), mask_t.astype(BF16)], axis=0)

    def logits_into(j, start, dst_ref):
        k_aug = jnp.concatenate([k_ref[0, pl.ds(start, tk), :], lanes_ref[pl.ds(start, tk), :]], axis=1)
        s_t = jnp.dot(k_aug, q_aug_t, preferred_element_type=F32)
        dst_ref[...] = s_t + _bias_block(bias_ref, i * (tq // ATTN_TILE), j * (tk // ATTN_TILE), n_bias)

    acc = _causal_sweep(i, logits_into, vt_ref, (sa_ref, sb_ref), m_ref, acc_ref)
    o_ref[0] = (acc[:HEAD_DIM] / acc[HEAD_DIM:HEAD_DIM + 1]).T.astype(o_ref.dtype)


def _moba_attn(qkv, bias_t):
    b, s, _ = qkv.shape
    n_bias = bias_t.shape[1] - 1
    n_blocks = s // MOBA_BLOCK
    assert s % KV_TILE == 0 and s % Q_TILE == 0 and n_blocks <= HEAD_DIM and n_blocks % 8 == 0
    block_lanes = np.zeros((s, HEAD_DIM), np.float32)
    block_lanes[np.arange(s), np.arange(s) // MOBA_BLOCK] = 1.0
    kern = functools.partial(_moba_kernel, n_bias=n_bias, n_blocks=n_blocks, seq_len=s)
    return pl.pallas_call(
        kern,
        grid=(b, N_HEADS_MOBA, s // Q_TILE),
        in_specs=[
            pl.BlockSpec((1, Q_TILE, HEAD_DIM), lambda bi, h, i: (bi, i, _COL_QB + h)),
            pl.BlockSpec((1, s, HEAD_DIM), lambda bi, h, i: (bi, 0, _COL_KB + h)),
            pl.BlockSpec((1, s, HEAD_DIM), lambda bi, h, i: (bi, 0, _COL_VB + h)),
            pl.BlockSpec((s, HEAD_DIM), lambda bi, h, i: (0, 0)),
            pl.BlockSpec((1, n_bias + 1, ATTN_TILE, ATTN_TILE), lambda bi, h, i: (N_HEADS_DIFF + h, 0, 0, 0)),
        ],
        out_specs=pl.BlockSpec((1, Q_TILE, HEAD_DIM), lambda bi, h, i: (bi, i, h)),
        out_shape=jax.ShapeDtypeStruct((b, s, W_MOBA), BF16),
        scratch_shapes=_sweep_scratch(s, Q_TILE) + [pltpu.VMEM((n_blocks, HEAD_DIM), F32)],
        compiler_params=_params(("parallel", "parallel", "arbitrary")),
        name="moba_attn",
    )(qkv, qkv, qkv, jnp.asarray(block_lanes, BF16), bias_t)


def _dil_kernel(q_ref, k_ref, v_ref, bias_ref, o_ref, qf_ref, kf_ref, vf_ref, m_ref, l_ref, acc_ref,
                *, seq_len):
    blk, nb = DIL_BLOCK, DIL_BATCH
    rows = nb * blk
    chunk = KV_TILE

    kf_ref[0:DIL_PAD, :] = jnp.zeros((DIL_PAD, HEAD_DIM), F32)
    vf_ref[0:DIL_PAD, :] = jnp.zeros((DIL_PAD, HEAD_DIM), F32)

    def widen(c, carry):
        src = pl.ds(pl.multiple_of(c * chunk, chunk), chunk)
        dst = pl.ds(pl.multiple_of(DIL_PAD + c * chunk, chunk), chunk)
        qf_ref[src, :] = q_ref[0, src, :].astype(F32)
        kf_ref[dst, :] = k_ref[0, src, :].astype(F32)
        vf_ref[dst, :] = v_ref[0, src, :].astype(F32)
        return carry
    lax.fori_loop(0, seq_len // chunk, widen, 0)

    for br, (_, dil) in enumerate(DIL_PAIRS):
        trips_per_res = seq_len // dil // rows

        def rows_of(start, n, dil=dil):
            return pl.ds(start, n, stride=dil) if dil > 1 else pl.ds(start, n)

        def trip(idx, carry, br=br, dil=dil, trips_per_res=trips_per_res, rows_of=rows_of):
            res = idx // trips_per_res
            t = idx % trips_per_res
            tok = res + t * rows * dil
            q = qf_ref[rows_of(tok, rows), :].astype(BF16).reshape(nb, blk, HEAD_DIM)
            kk = kf_ref[rows_of(DIL_PAD + tok - blk * dil, rows + blk), :].astype(BF16)
            vv = vf_ref[rows_of(DIL_PAD + tok - blk * dil, rows + blk), :].astype(BF16)
            k_prev, k_cur = kk[:rows].reshape(nb, blk, HEAD_DIM), kk[blk:].reshape(nb, blk, HEAD_DIM)
            v_prev, v_cur = vv[:rows].reshape(nb, blk, HEAD_DIM), vv[blk:].reshape(nb, blk, HEAD_DIM)
            b_first = bias_ref[br, 0, jnp.where(t == 0, 2, 0)]
            b_prev = jnp.concatenate(
                [b_first[None], jnp.broadcast_to(bias_ref[br, 0, 0][None], (nb - 1, blk, blk))], axis=0)
            s_prev = jnp.einsum("nqd,nkd->nqk", q, k_prev, preferred_element_type=F32) + b_prev
            s_cur = jnp.einsum("nqd,nkd->nqk", q, k_cur, preferred_element_type=F32) + bias_ref[br, 0, 1][None]
            m_b = jnp.maximum(jnp.max(s_prev, axis=-1, keepdims=True), jnp.max(s_cur, axis=-1, keepdims=True))
            e_prev = jnp.exp(s_prev - m_b)
            e_cur = jnp.exp(s_cur - m_b)
            l_b = jnp.sum(e_prev, axis=-1, keepdims=True) + jnp.sum(e_cur, axis=-1, keepdims=True)
            o_b = (jnp.einsum("nqk,nkd->nqd", e_prev.astype(BF16), v_prev, preferred_element_type=F32)
                   + jnp.einsum("nqk,nkd->nqd", e_cur.astype(BF16), v_cur, preferred_element_type=F32))
            m_b = jnp.broadcast_to(m_b.reshape(rows, 1), (rows, HEAD_DIM))
            l_b = jnp.broadcast_to(l_b.reshape(rows, 1), (rows, HEAD_DIM))
            o_b = o_b.reshape(rows, HEAD_DIM)
            dst = rows_of(tok, rows)
            if br == 0:
                m_ref[dst, :], l_ref[dst, :], acc_ref[dst, :] = m_b, l_b, o_b
            else:
                m_old = m_ref[dst, :]
                m_new = jnp.maximum(m_old, m_b)
                w_old = jnp.exp(m_old - m_new)
                w_b = jnp.exp(m_b - m_new)
                m_ref[dst, :] = m_new
                l_ref[dst, :] = w_old * l_ref[dst, :] + w_b * l_b
                acc_ref[dst, :] = w_old * acc_ref[dst, :] + w_b * o_b
            return carry
        lax.fori_loop(0, dil * trips_per_res, trip, 0)

    def finish(c, carry):
        src = pl.ds(pl.multiple_of(c * chunk, chunk), chunk)
        o_ref[0, src, :] = (acc_ref[src, :] / l_ref[src, :]).astype(o_ref.dtype)
        return carry
    lax.fori_loop(0, seq_len // chunk, finish, 0)


def _dil_attn(qkv, bias):
    b, s, _ = qkv.shape
    n_br = len(DIL_PAIRS)
    assert all(s % (dil * DIL_BATCH * DIL_BLOCK) == 0 for _, dil in DIL_PAIRS) and s % KV_TILE == 0
    slab = lambda col: pl.BlockSpec((1, s, HEAD_DIM), lambda bi, h: (bi, 0, col + h))
    return pl.pallas_call(
        functools.partial(_dil_kernel, seq_len=s),
        grid=(b, N_HEADS_DIL),
        in_specs=[slab(_COL_QC), slab(_COL_QC + N_HEADS_DIL), slab(_COL_QC + 2 * N_HEADS_DIL),
                  pl.BlockSpec((n_br, 1, 3, DIL_BLOCK, DIL_BLOCK), lambda bi, h: (0, h, 0, 0, 0))],
        out_specs=pl.BlockSpec((1, s, HEAD_DIM), lambda bi, h: (bi, 0, h)),
        out_shape=jax.ShapeDtypeStruct((b, s, W_DIL), BF16),
        scratch_shapes=[pltpu.VMEM((s, HEAD_DIM), F32), pltpu.VMEM((DIL_PAD + s, HEAD_DIM), F32),
                        pltpu.VMEM((DIL_PAD + s, HEAD_DIM), F32), pltpu.VMEM((s, HEAD_DIM), F32),
                        pltpu.VMEM((s, HEAD_DIM), F32), pltpu.VMEM((s, HEAD_DIM), F32)],
        compiler_params=_params(("parallel", "parallel")),
        name="dil_attn",
    )(qkv, qkv, qkv, bias)


def _out_proj_kernel(x_ref, a_ref, b_ref, c_ref, w_ref, o_ref):
    acc = jnp.dot(a_ref[...], w_ref[0:W_DIFF, :], preferred_element_type=F32)
    acc += jnp.dot(b_ref[...], w_ref[W_DIFF:W_DIFF + W_MOBA, :], preferred_element_type=F32)
    acc += jnp.dot(c_ref[...], w_ref[W_DIFF + W_MOBA:, :], preferred_element_type=F32)
    o_ref[...] = x_ref[...] + acc


def _out_proj(x, o_a, o_b, o_c, w_out, layer, *, tm, tn):
    t, d = x.shape
    row = lambda width: pl.BlockSpec((tm, width), lambda i, j: (i, 0))
    return pl.pallas_call(
        _out_proj_kernel,
        grid=(t // tm, d // tn),
        in_specs=[pl.BlockSpec((tm, tn), lambda i, j: (i, j)), row(W_DIFF), row(W_MOBA), row(W_DIL),
                  pl.BlockSpec((None, MIX_WIDTH, tn), lambda i, j: (layer, 0, j))],
        out_specs=pl.BlockSpec((tm, tn), lambda i, j: (i, j)),
        out_shape=jax.ShapeDtypeStruct((t, d), F32),
        compiler_params=_params(("parallel", "parallel")),
        name="out_proj",
    )(x, o_a, o_b, o_c, w_out)


def _cross_attn_kernel(x_ref, g_ref, wq_ref, kv_ref, wo_ref, o_ref):
    x = x_ref[...]
    hn = _rms_norm(x, g_ref[...]).astype(BF16)
    q = jnp.dot(hn, wq_ref[...], preferred_element_type=F32).astype(BF16)
    kv = kv_ref[0]
    heads = []
    for h in range(N_MEM_HEADS):
        lo = h * HEAD_DIM
        s = lax.dot_general(q[:, lo:lo + HEAD_DIM], kv[:, lo:lo + HEAD_DIM], _NT,
                            preferred_element_type=F32)
        e = jnp.exp(s - jnp.max(s, axis=-1, keepdims=True))
        oh = jnp.dot(e.astype(BF16), kv[:, MEM_WIDTH + lo:MEM_WIDTH + lo + HEAD_DIM],
                     preferred_element_type=F32)
        heads.append((oh / jnp.sum(e, axis=-1, keepdims=True)).astype(BF16))
    o = jnp.concatenate(heads, axis=1)
    o_ref[...] = x + jnp.dot(o, wo_ref[...], preferred_element_type=F32)


def _cross_attn(x, g, w_cq, kv, w_co, layer, seq_len, *, tm):
    t, d = x.shape
    n_mem = kv.shape[1]
    tiles_per_batch = seq_len // tm
    return pl.pallas_call(
        _cross_attn_kernel,
        grid=(t // tm,),
        in_specs=[
            pl.BlockSpec((tm, d), lambda i: (i, 0)),
            pl.BlockSpec((None, 1, d), lambda i: (layer, 0, 0)),
            pl.BlockSpec((None, d, MEM_WIDTH), lambda i: (layer, 0, 0)),
            pl.BlockSpec((1, n_mem, 2 * MEM_WIDTH), lambda i: (i // tiles_per_batch, 0, 0)),
            pl.BlockSpec((None, MEM_WIDTH, d), lambda i: (layer, 0, 0)),
        ],
        out_specs=pl.BlockSpec((tm, d), lambda i: (i, 0)),
        out_shape=jax.ShapeDtypeStruct((t, d), F32),
        compiler_params=_params(("parallel",)),
        name="cross_attn",
    )(x, g, w_cq, kv, w_co)


def _ffn_kernel(x_ref, xp_ref, g_ref, wg_ref, wu_ref, cw_ref, cb_ref, wd_ref, gf_ref, o_ref, hn_ref,
                *, tm, tiles_per_batch, final_norm):
    i = pl.program_id(0)
    f = pl.program_id(1)

    @pl.when(f == 0)
    def _():
        g = g_ref[...]
        hn_ref[FFN_HALO:, :] = _rms_norm(x_ref[...], g).astype(BF16)
        prev = _rms_norm(xp_ref[...], g)
        prev = jnp.where(i % tiles_per_batch == 0, 0.0, prev)
        hn_ref[0:FFN_HALO, :] = prev.astype(BF16)
        o_ref[...] = x_ref[...]

    gate = jnp.dot(hn_ref[...], wg_ref[...], preferred_element_type=F32)
    up = jnp.dot(hn_ref[FFN_HALO:, :], wu_ref[...], preferred_element_type=F32)
    cw = cw_ref[...]
    conv = (cw[0:1] * gate[FFN_HALO - 2:FFN_HALO - 2 + tm]
            + cw[1:2] * gate[FFN_HALO - 1:FFN_HALO - 1 + tm]
            + cw[2:3] * gate[FFN_HALO:]) + cb_ref[...]
    act = conv * jax.nn.sigmoid(conv) * up
    o_ref[...] += jnp.dot(act.astype(BF16), wd_ref[...], preferred_element_type=F32)

    if final_norm:
        @pl.when(f == pl.num_programs(1) - 1)
        def _():
            o_ref[...] = _rms_norm(o_ref[...], gf_ref[...])


def _conv_ffn(x, g, w_up, conv_w, conv_b, w_down, g_final, layer, seq_len, *, tm, tf, final_norm):
    t, d = x.shape
    n_f = D_FF // tf
    tiles_per_batch = seq_len // tm
    halo_blocks = tm // FFN_HALO
    kern = functools.partial(_ffn_kernel, tm=tm, tiles_per_batch=tiles_per_batch, final_norm=final_norm)
    return pl.pallas_call(
        kern,
        grid=(t // tm, n_f),
        in_specs=[
            pl.BlockSpec((tm, d), lambda i, f: (i, 0)),
            pl.BlockSpec((FFN_HALO, d), lambda i, f: (jnp.maximum(i * halo_blocks - 1, 0), 0)),
            pl.BlockSpec((None, 1, d), lambda i, f: (layer, 0, 0)),
            pl.BlockSpec((None, d, tf), lambda i, f: (layer, 0, f)),
            pl.BlockSpec((None, d, tf), lambda i, f: (layer, 0, n_f + f)),
            pl.BlockSpec((None, CONV_WIDTH, tf), lambda i, f: (layer, 0, f)),
            pl.BlockSpec((None, 1, tf), lambda i, f: (layer, 0, f)),
            pl.BlockSpec((None, tf, d), lambda i, f: (layer, f, 0)),
            pl.BlockSpec((1, d), lambda i, f: (0, 0)),
        ],
        out_specs=pl.BlockSpec((tm, d), lambda i, f: (i, 0)),
        out_shape=jax.ShapeDtypeStruct((t, d), F32),
        scratch_shapes=[pltpu.VMEM((FFN_HALO + tm, d), BF16)],
        compiler_params=_params(("parallel", "arbitrary")),
        name="conv_ffn",
    )(x, x, g, w_up, w_up, conv_w, conv_b, w_down, g_final)


def _t5_bucket(dist):
    n = jnp.maximum(dist, 0)
    max_exact = N_BUCKETS // 2
    nf = jnp.maximum(n, 1).astype(F32)
    log_ratio = jnp.log(nf / max_exact) / math.log(REL_MAX_DIST / max_exact)
    large = max_exact + (log_ratio * (N_BUCKETS - max_exact)).astype(jnp.int32)
    large = jnp.minimum(large, N_BUCKETS - 1)
    return jnp.where(n < max_exact, n, large)


def _toeplitz(w, rows, cols, offset):
    length = cols + offset + 1
    assert w.shape[-1] == length and offset >= rows - 1
    lead = w.shape[:-1]
    flat = jnp.broadcast_to(w[..., None, :], lead + (rows, length)).reshape(lead + (rows * length,))
    shifted = flat[..., :rows * (length - 1)].reshape(lead + (rows, length - 1))
    return shifted[..., offset:offset + cols]


def _causal_bias_tiles(table, tile):
    n_tiles = -(-(REL_MAX_DIST + tile - 1) // tile) + 1
    dist = np.arange(-(tile - 1), n_tiles * tile + 1, dtype=np.int32)
    vec = jnp.where(dist[None, :] >= 0, table.astype(F32)[_t5_bucket(jnp.asarray(dist))].T, NEG_INF)
    tiles = _toeplitz(vec, tile, n_tiles * tile, tile - 1)
    tiles = tiles.reshape(-1, tile, n_tiles, tile).transpose(0, 2, 1, 3)
    return jnp.concatenate([tiles, jnp.full((tiles.shape[0], 1, tile, tile), NEG_INF, F32)], axis=1)


def _dilated_bias_tiles(table, window, dil):
    blk = DIL_BLOCK
    steps = window // dil
    j = np.arange(-(blk - 1), 2 * blk + 1, dtype=np.int32)
    valid = (j >= 0) & (j <= steps)
    bias = table.astype(F32)[_t5_bucket(jnp.asarray(np.maximum(j, 0) * dil))].T
    vec = jnp.where(valid[None, :], bias, NEG_INF)
    tiles = _toeplitz(vec, blk, 2 * blk, blk - 1)
    tiles = tiles.reshape(-1, blk, 2, blk).transpose(0, 2, 3, 1)
    masked = jnp.full((tiles.shape[0], 1, blk, blk), NEG_INF, F32)
    return jnp.concatenate([tiles[:, 1:2], tiles[:, 0:1], masked], axis=1)


def kernel(x, mem, w_in, w_out, g_mix, diff_lambda, diff_subln, rel_bias_table, g_cross, g_mem,
           w_cq, w_ckv, w_co, g_ffn, w_up, conv_w, conv_b, w_down, g_final):
    b, s, d = x.shape
    depth = w_in.shape[0]
    n_mem = mem.shape[1]
    t = b * s

    col_scale = np.ones((3 * MIX_WIDTH,), np.float32)
    col_scale[_COL_QA * HEAD_DIM:_COL_KA * HEAD_DIM] = DIFF_QK_DIM ** -0.5 * LOG2E
    col_scale[_COL_QB * HEAD_DIM:_COL_KB * HEAD_DIM] = HEAD_DIM ** -0.5 * LOG2E
    col_scale[_COL_QC * HEAD_DIM:(_COL_QC + N_HEADS_DIL) * HEAD_DIM] = HEAD_DIM ** -0.5
    w_in_b = (w_in * col_scale).astype(BF16)
    w_out_b = w_out.astype(BF16)
    w_cq_b = (w_cq * HEAD_DIM ** -0.5).astype(BF16)
    w_ckv_b = w_ckv.astype(BF16)
    w_co_b = w_co.astype(BF16)
    w_up_b = w_up.astype(BF16)
    w_down_b = w_down.astype(BF16)

    g_mix3 = g_mix.reshape(depth, 1, d)
    g_cross3 = g_cross.reshape(depth, 1, d)
    g_mem3 = g_mem.reshape(depth, 1, d)
    g_ffn3 = g_ffn.reshape(depth, 1, d)
    subln_col = diff_subln.reshape(depth, HEAD_DIM, 1)
    conv_b3 = conv_b.reshape(depth, 1, D_FF)
    g_final2 = g_final.reshape(1, d)

    n_ab = N_HEADS_DIFF + N_HEADS_MOBA
    bias_ab = _causal_bias_tiles(rel_bias_table[:, :n_ab] * LOG2E, ATTN_TILE)
    bias_c = jnp.stack([_dilated_bias_tiles(rel_bias_table[:, n_ab:], window, dil)
                        for window, dil in DIL_PAIRS])

    xf = x.reshape(t, d)
    memf = mem.reshape(b * n_mem, d)

    for layer in range(depth):
        qkv = _norm_matmul(xf, g_mix3, w_in_b, layer, tm=1024, tn=2048).reshape(b, s, 3 * MIX_WIDTH)
        o_a = _diff_attn(qkv, bias_ab, diff_lambda, subln_col, layer)
        o_b = _moba_attn(qkv, bias_ab)
        o_c = _dil_attn(qkv, bias_c)
        xf = _out_proj(xf, o_a.reshape(t, W_DIFF), o_b.reshape(t, W_MOBA), o_c.reshape(t, W_DIL),
                       w_out_b, layer, tm=1024, tn=1024)
        kv = _norm_matmul(memf, g_mem3, w_ckv_b, layer, tm=b * n_mem, tn=512)
        xf = _cross_attn(xf, g_cross3, w_cq_b, kv.reshape(b, n_mem, 2 * MEM_WIDTH), w_co_b, layer, s,
                         tm=512)
        xf = _conv_ffn(xf, g_ffn3, w_up_b, conv_w, conv_b3, w_down_b, g_final2, layer, s,
                       tm=512, tf=512, final_norm=(layer == depth - 1))
    return xf.reshape(b, s, d)
```

```python
import functools
import math

import numpy as np
import jax
import jax.numpy as jnp
from jax import lax
from jax.experimental import pallas as pl
from jax.experimental.pallas import tpu as pltpu

F32 = jnp.float32
BF16 = jnp.bfloat16

D_MODEL = 2048
HEAD_DIM = 128
N_HEADS_DIFF = 6
N_HEADS_MOBA = 4
N_HEADS_DIL = 6
W_DIFF = N_HEADS_DIFF * HEAD_DIM
W_MOBA = N_HEADS_MOBA * HEAD_DIM
W_DIL = N_HEADS_DIL * HEAD_DIM
MIX_WIDTH = W_DIFF + W_MOBA + W_DIL
DIFF_QK_DIM = HEAD_DIM // 2
MOBA_BLOCK = 256
MOBA_TOPK = 3
DIL_PAIRS = ((128, 1), (512, 4), (2048, 16))
DIL_BLOCK = 128
N_BUCKETS = 32
REL_MAX_DIST = 2048
N_MEM_HEADS = 4
MEM_WIDTH = N_MEM_HEADS * HEAD_DIM
D_FF = 5632
CONV_WIDTH = 3
NORM_EPS = 1e-6
NEG_INF = -1e30

_COL_QA = 0
_COL_KA = _COL_QA + N_HEADS_DIFF
_COL_VA = _COL_KA + N_HEADS_DIFF
_COL_QB = _COL_VA + N_HEADS_DIFF
_COL_KB = _COL_QB + N_HEADS_MOBA
_COL_VB = _COL_KB + N_HEADS_MOBA
_COL_QC = _COL_VB + N_HEADS_MOBA
_QKV_COLS = 3 * MIX_WIDTH // HEAD_DIM

ATTN_TILE = 256
Q_TILE = 1024
KV_TILE = 1024
VT_ROWS = HEAD_DIM + 16
LOG2E = math.log2(math.e)
DIL_BATCH = 4
DIL_PAD = DIL_BLOCK * max(dil for _, dil in DIL_PAIRS)
FFN_HALO = 16
VMEM_LIMIT = 56 * 1024 * 1024

_NT = (((1,), (1,)), ((), ()))


def _params(semantics):
    return pltpu.CompilerParams(dimension_semantics=semantics, vmem_limit_bytes=VMEM_LIMIT)


def _rms_norm(xf, g):
    y = xf * lax.rsqrt(jnp.mean(xf * xf, axis=-1, keepdims=True) + NORM_EPS)
    return y * g


def _norm_matmul_kernel(x_ref, g_ref, w_ref, o_ref, hn_ref):
    @pl.when(pl.program_id(1) == 0)
    def _():
        hn_ref[...] = _rms_norm(x_ref[...], g_ref[...]).astype(BF16)

    o_ref[...] = jnp.dot(hn_ref[...], w_ref[...], preferred_element_type=F32).astype(o_ref.dtype)


def _norm_matmul(x, g, w, layer, *, tm, tn):
    t, d = x.shape
    n = w.shape[-1]
    return pl.pallas_call(
        _norm_matmul_kernel,
        grid=(t // tm, n // tn),
        in_specs=[
            pl.BlockSpec((tm, d), lambda i, j: (i, 0)),
            pl.BlockSpec((None, 1, d), lambda i, j: (layer, 0, 0)),
            pl.BlockSpec((None, d, tn), lambda i, j: (layer, 0, j)),
        ],
        out_specs=pl.BlockSpec((tm, tn), lambda i, j: (i, j)),
        out_shape=jax.ShapeDtypeStruct((t, n), BF16),
        scratch_shapes=[pltpu.VMEM((tm, d), BF16)],
        compiler_params=_params(("parallel", "arbitrary")),
        name="norm_matmul",
    )(x, g, w)


def _bias_block(bias_ref, q_blk, k_blk, n_bias):
    cols = []
    for e in range(Q_TILE // ATTN_TILE):
        tiles = []
        for c in range(KV_TILE // ATTN_TILE):
            d = (q_blk + e) - (k_blk + c)
            tiles.append(bias_ref[0, jnp.where(d < 0, n_bias, jnp.minimum(d, n_bias - 1))])
        cols.append(jnp.concatenate(tiles, axis=0))
    return jnp.concatenate(cols, axis=1)


def _causal_sweep(i, logits, vt_ref, s_refs, m_ref, acc_ref):
    tk = KV_TILE
    n_steps = ((i + 1) * Q_TILE - 1) // tk + 1

    def key_start(j):
        return pl.multiple_of(jnp.minimum(j, n_steps - 1) * tk, tk)

    def logits_into(j, dst_ref):
        s_t = logits(j, key_start(j))
        dst_ref[0:tk, :] = s_t
        dst_ref[tk:tk + 1, :] = jnp.max(s_t, axis=0, keepdims=True)

    def step(j, slot):
        logits_into(j + 1, s_refs[1 - slot])
        m = m_ref[...]
        m_new = jnp.maximum(m, s_refs[slot][tk:tk + 1, :])
        p = jnp.exp2((s_refs[slot][0:tk, :] - m_new).astype(BF16))
        pv = jnp.dot(vt_ref[:, pl.ds(key_start(j), tk)], p, preferred_element_type=F32)
        acc_ref[...] = jnp.exp2(m - m_new) * acc_ref[...] + pv
        m_ref[...] = m_new

    m_ref[...] = jnp.full(m_ref.shape, NEG_INF, F32)
    acc_ref[...] = jnp.zeros(acc_ref.shape, F32)
    logits_into(0, s_refs[0])

    def body(j, carry):
        for slot in range(2):
            @pl.when(j % 2 == slot)
            def _(slot=slot):
                step(j, slot)
        return carry

    lax.fori_loop(0, n_steps, body, 0)
    return acc_ref[...]


def _sweep_scratch(seq_len, n):
    return [pltpu.VMEM((VT_ROWS, seq_len), BF16), pltpu.VMEM((KV_TILE + 8, n), F32),
            pltpu.VMEM((KV_TILE + 8, n), F32), pltpu.VMEM((1, n), F32), pltpu.VMEM((VT_ROWS, n), F32)]


def _transpose_values(v_ref, vt_ref, seq_len):
    def body(c, carry):
        start = pl.multiple_of(c * KV_TILE, KV_TILE)
        vt_ref[0:HEAD_DIM, pl.ds(start, KV_TILE)] = (
            v_ref[0, pl.ds(start, KV_TILE), :].astype(F32).T.astype(BF16))
        return carry
    lax.fori_loop(0, seq_len // KV_TILE, body, 0)
    vt_ref[HEAD_DIM:, :] = jnp.ones((VT_ROWS - HEAD_DIM, seq_len), BF16)


def _diff_attn_kernel(q_ref, k_ref, v_ref, bias_ref, lam_ref, sg_ref, o_ref, vt_ref, sa_ref, sb_ref,
                      m_ref, acc_ref, *, n_bias, lam_init, seq_len):
    tq, tk = Q_TILE, KV_TILE
    i = pl.program_id(2)

    @pl.when(i == 0)
    def _():
        _transpose_values(v_ref, vt_ref, seq_len)

    q_t = q_ref[0].astype(F32).T
    row = lax.broadcasted_iota(jnp.int32, q_t.shape, 0)
    qm_t = jnp.concatenate([jnp.where(row < DIFF_QK_DIM, q_t, 0.0),
                            jnp.where(row >= DIFF_QK_DIM, q_t, 0.0)], axis=1).astype(BF16)

    def logits(j, start):
        s_t = jnp.dot(k_ref[0, pl.ds(start, tk), :], qm_t, preferred_element_type=F32)
        b_t = _bias_block(bias_ref, i * (tq // ATTN_TILE), j * (tk // ATTN_TILE), n_bias)
        return s_t + jnp.concatenate([b_t, b_t], axis=1)

    acc = _causal_sweep(i, logits, vt_ref, (sa_ref, sb_ref), m_ref, acc_ref)
    o_t = acc[:HEAD_DIM] / acc[HEAD_DIM:HEAD_DIM + 1]
    lp = lam_ref[...]
    lam = (jnp.exp(jnp.sum(lp[0:1] * lp[1:2], keepdims=True))
           - jnp.exp(jnp.sum(lp[2:3] * lp[3:4], keepdims=True)) + lam_init)
    od_t = o_t[:, :tq] - lam * o_t[:, tq:]
    y_t = od_t * lax.rsqrt(jnp.mean(od_t * od_t, axis=0, keepdims=True) + NORM_EPS) * sg_ref[...]
    o_ref[0] = (y_t * (1.0 - lam_init)).T.astype(o_ref.dtype)


def _diff_attn(qkv, bias_t, diff_lambda, diff_subln_col, layer):
    b, s, _ = qkv.shape
    n_bias = bias_t.shape[1] - 1
    assert s % KV_TILE == 0 and s % Q_TILE == 0
    lam_init = 0.8 - 0.6 * math.exp(-0.3 * layer)
    kern = functools.partial(_diff_attn_kernel, n_bias=n_bias, lam_init=lam_init, seq_len=s)
    return pl.pallas_call(
        kern,
        grid=(b, N_HEADS_DIFF, s // Q_TILE),
        in_specs=[
            pl.BlockSpec((1, Q_TILE, HEAD_DIM), lambda bi, h, i: (bi, i, _COL_QA + h)),
            pl.BlockSpec((1, s, HEAD_DIM), lambda bi, h, i: (bi, 0, _COL_KA + h)),
            pl.BlockSpec((1, s, HEAD_DIM), lambda bi, h, i: (bi, 0, _COL_VA + h)),
            pl.BlockSpec((1, n_bias + 1, ATTN_TILE, ATTN_TILE), lambda bi, h, i: (h, 0, 0, 0)),
            pl.BlockSpec((None, 4, DIFF_QK_DIM), lambda bi, h, i: (layer, 0, 0)),
            pl.BlockSpec((None, HEAD_DIM, 1), lambda bi, h, i: (layer, 0, 0)),
        ],
        out_specs=pl.BlockSpec((1, Q_TILE, HEAD_DIM), lambda bi, h, i: (bi, i, h)),
        out_shape=jax.ShapeDtypeStruct((b, s, W_DIFF), BF16),
        scratch_shapes=_sweep_scratch(s, 2 * Q_TILE),
        compiler_params=_params(("parallel", "parallel", "arbitrary")),
        name="diff_attn",
    )(qkv, qkv, qkv, bias_t, diff_lambda, diff_subln_col)


def _moba_kernel(q_ref, k_ref, v_ref, lanes_ref, bias_ref, o_ref, vt_ref, sa_ref, sb_ref, m_ref, acc_ref,
                 kmean_ref, *, n_bias, n_blocks, seq_len):
    tq, tk = Q_TILE, KV_TILE
    i = pl.program_id(2)

    @pl.when(i == 0)
    def _():
        _transpose_values(v_ref, vt_ref, seq_len)
        kf = k_ref[0].astype(F32).reshape(n_blocks, MOBA_BLOCK, HEAD_DIM)
        kmean_ref[...] = jnp.mean(kf, axis=1)

    q_t = q_ref[0].astype(F32).T
    gate = jnp.dot(kmean_ref[...], q_t, preferred_element_type=F32,
                   precision=lax.Precision.HIGHEST)
    blk = lax.broadcasted_iota(jnp.int32, gate.shape, 0)
    lane = lax.broadcasted_iota(jnp.int32, gate.shape, 1)
    own = i * (tq // MOBA_BLOCK)
    for e in range(1, tq // MOBA_BLOCK):
        own = own + (lane >= e * MOBA_BLOCK).astype(jnp.int32)
    past = blk < own
    gm = jnp.where(past, gate, NEG_INF)
    rank = jnp.zeros(gate.shape, jnp.int32)
    for jp in range(n_blocks):
        gj = gm[jp:jp + 1, :]
        beats = (gj > gm) | ((gj == gm) & (jp < blk))
        rank = rank + beats.astype(jnp.int32)
    allowed = (past & (rank < MOBA_TOPK)) | (blk == own)
    mask_t = jnp.where(allowed, 0.0, NEG_INF)
    mask_t = jnp.concatenate([mask_t, jnp.zeros((HEAD_DIM - n_blocks, tq), F32)], axis=0)
    q_aug_t = jnp.concatenate([q_t.astype(BF16), mask_t.astype(BF16)], axis=0)

    def logits(j, start):
        k_aug = jnp.concatenate([k_ref[0, pl.ds(start, tk), :], lanes_ref[pl.ds(start, tk), :]], axis=1)
        s_t = jnp.dot(k_aug, q_aug_t, preferred_element_type=F32)
        return s_t + _bias_block(bias_ref, i * (tq // ATTN_TILE), j * (tk // ATTN_TILE), n_bias)

    acc = _causal_sweep(i, logits, vt_ref, (sa_ref, sb_ref), m_ref, acc_ref)
    o_ref[0] = (acc[:HEAD_DIM] / acc[HEAD_DIM:HEAD_DIM + 1]).T.astype(o_ref.dtype)


def _moba_attn(qkv, bias_t):
    b, s, _ = qkv.shape
    n_bias = bias_t.shape[1] - 1
    n_blocks = s // MOBA_BLOCK
    assert s % KV_TILE == 0 and s % Q_TILE == 0 and n_blocks <= HEAD_DIM and n_blocks % 8 == 0
    block_lanes = np.zeros((s, HEAD_DIM), np.float32)
    block_lanes[np.arange(s), np.arange(s) // MOBA_BLOCK] = 1.0
    kern = functools.partial(_moba_kernel, n_bias=n_bias, n_blocks=n_blocks, seq_len=s)
    return pl.pallas_call(
        kern,
        grid=(b, N_HEADS_MOBA, s // Q_TILE),
        in_specs=[
            pl.BlockSpec((1, Q_TILE, HEAD_DIM), lambda bi, h, i: (bi, i, _COL_QB + h)),
            pl.BlockSpec((1, s, HEAD_DIM), lambda bi, h, i: (bi, 0, _COL_KB + h)),
            pl.BlockSpec((1, s, HEAD_DIM), lambda bi, h, i: (bi, 0, _COL_VB + h)),
            pl.BlockSpec((s, HEAD_DIM), lambda bi, h, i: (0, 0)),
            pl.BlockSpec((1, n_bias + 1, ATTN_TILE, ATTN_TILE), lambda bi, h, i: (N_HEADS_DIFF + h, 0, 0, 0)),
        ],
        out_specs=pl.BlockSpec((1, Q_TILE, HEAD_DIM), lambda bi, h, i: (bi, i, h)),
        out_shape=jax.ShapeDtypeStruct((b, s, W_MOBA), BF16),
        scratch_shapes=_sweep_scratch(s, Q_TILE) + [pltpu.VMEM((n_blocks, HEAD_DIM), F32)],
        compiler_params=_params(("parallel", "parallel", "arbitrary")),
        name="moba_attn",
    )(qkv, qkv, qkv, jnp.asarray(block_lanes, BF16), bias_t)


def _dil_kernel(q_ref, k_ref, v_ref, bias_ref, o_ref, qf_ref, kf_ref, vf_ref, m_ref, l_ref, acc_ref,
                *, seq_len):
    blk, nb = DIL_BLOCK, DIL_BATCH
    rows = nb * blk
    chunk = KV_TILE

    kf_ref[0:DIL_PAD, :] = jnp.zeros((DIL_PAD, HEAD_DIM), F32)
    vf_ref[0:DIL_PAD, :] = jnp.zeros((DIL_PAD, HEAD_DIM), F32)

    def widen(c, carry):
        src = pl.ds(pl.multiple_of(c * chunk, chunk), chunk)
        dst = pl.ds(pl.multiple_of(DIL_PAD + c * chunk, chunk), chunk)
        qf_ref[src, :] = q_ref[0, src, :].astype(F32)
        kf_ref[dst, :] = k_ref[0, src, :].astype(F32)
        vf_ref[dst, :] = v_ref[0, src, :].astype(F32)
        return carry
    lax.fori_loop(0, seq_len // chunk, widen, 0)

    for br, (_, dil) in enumerate(DIL_PAIRS):
        trips_per_res = seq_len // dil // rows

        def rows_of(start, n, dil=dil):
            return pl.ds(start, n, stride=dil) if dil > 1 else pl.ds(start, n)

        def trip(idx, carry, br=br, dil=dil, trips_per_res=trips_per_res, rows_of=rows_of):
            res = idx // trips_per_res
            t = idx % trips_per_res
            tok = res + t * rows * dil
            q = qf_ref[rows_of(tok, rows), :].astype(BF16).reshape(nb, blk, HEAD_DIM)
            kk = kf_ref[rows_of(DIL_PAD + tok - blk * dil, rows + blk), :].astype(BF16)
            vv = vf_ref[rows_of(DIL_PAD + tok - blk * dil, rows + blk), :].astype(BF16)
            k_prev, k_cur = kk[:rows].reshape(nb, blk, HEAD_DIM), kk[blk:].reshape(nb, blk, HEAD_DIM)
            v_prev, v_cur = vv[:rows].reshape(nb, blk, HEAD_DIM), vv[blk:].reshape(nb, blk, HEAD_DIM)
            b_first = bias_ref[br, 0, jnp.where(t == 0, 2, 0)]
            b_prev = jnp.concatenate(
                [b_first[None], jnp.broadcast_to(bias_ref[br, 0, 0][None], (nb - 1, blk, blk))], axis=0)
            s_prev = jnp.einsum("nqd,nkd->nqk", q, k_prev, preferred_element_type=F32) + b_prev
            s_cur = jnp.einsum("nqd,nkd->nqk", q, k_cur, preferred_element_type=F32) + bias_ref[br, 0, 1][None]
            m_b = jnp.maximum(jnp.max(s_prev, axis=-1, keepdims=True), jnp.max(s_cur, axis=-1, keepdims=True))
            e_prev = jnp.exp(s_prev - m_b)
            e_cur = jnp.exp(s_cur - m_b)
            l_b = jnp.sum(e_prev, axis=-1, keepdims=True) + jnp.sum(e_cur, axis=-1, keepdims=True)
            o_b = (jnp.einsum("nqk,nkd->nqd", e_prev.astype(BF16), v_prev, preferred_element_type=F32)
                   + jnp.einsum("nqk,nkd->nqd", e_cur.astype(BF16), v_cur, preferred_element_type=F32))
            m_b = jnp.broadcast_to(m_b.reshape(rows, 1), (rows, HEAD_DIM))
            l_b = jnp.broadcast_to(l_b.reshape(rows, 1), (rows, HEAD_DIM))
            o_b = o_b.reshape(rows, HEAD_DIM)
            dst = rows_of(tok, rows)
            if br == 0:
                m_ref[dst, :], l_ref[dst, :], acc_ref[dst, :] = m_b, l_b, o_b
            else:
                m_old = m_ref[dst, :]
                m_new = jnp.maximum(m_old, m_b)
                w_old = jnp.exp(m_old - m_new)
                w_b = jnp.exp(m_b - m_new)
                m_ref[dst, :] = m_new
                l_ref[dst, :] = w_old * l_ref[dst, :] + w_b * l_b
                acc_ref[dst, :] = w_old * acc_ref[dst, :] + w_b * o_b
            return carry
        lax.fori_loop(0, dil * trips_per_res, trip, 0)

    def finish(c, carry):
        src = pl.ds(pl.multiple_of(c * chunk, chunk), chunk)
        o_ref[0, src, :] = (acc_ref[src, :] / l_ref[src, :]).astype(o_ref.dtype)
        return carry
    lax.fori_loop(0, seq_len // chunk, finish, 0)


def _dil_attn(qkv, bias):
    b, s, _ = qkv.shape
    n_br = len(DIL_PAIRS)
    assert all(s % (dil * DIL_BATCH * DIL_BLOCK) == 0 for _, dil in DIL_PAIRS) and s % KV_TILE == 0
    slab = lambda col: pl.BlockSpec((1, s, HEAD_DIM), lambda bi, h: (bi, 0, col + h))
    return pl.pallas_call(
        functools.partial(_dil_kernel, seq_len=s),
        grid=(b, N_HEADS_DIL),
        in_specs=[slab(_COL_QC), slab(_COL_QC + N_HEADS_DIL), slab(_COL_QC + 2 * N_HEADS_DIL),
                  pl.BlockSpec((n_br, 1, 3, DIL_BLOCK, DIL_BLOCK), lambda bi, h: (0, h, 0, 0, 0))],
        out_specs=pl.BlockSpec((1, s, HEAD_DIM), lambda bi, h: (bi, 0, h)),
        out_shape=jax.ShapeDtypeStruct((b, s, W_DIL), BF16),
        scratch_shapes=[pltpu.VMEM((s, HEAD_DIM), F32), pltpu.VMEM((DIL_PAD + s, HEAD_DIM), F32),
                        pltpu.VMEM((DIL_PAD + s, HEAD_DIM), F32), pltpu.VMEM((s, HEAD_DIM), F32),
                        pltpu.VMEM((s, HEAD_DIM), F32), pltpu.VMEM((s, HEAD_DIM), F32)],
        compiler_params=_params(("parallel", "parallel")),
        name="dil_attn",
    )(qkv, qkv, qkv, bias)


def _out_proj_kernel(x_ref, a_ref, b_ref, c_ref, w_ref, o_ref):
    acc = jnp.dot(a_ref[...], w_ref[0:W_DIFF, :], preferred_element_type=F32)
    acc += jnp.dot(b_ref[...], w_ref[W_DIFF:W_DIFF + W_MOBA, :], preferred_element_type=F32)
    acc += jnp.dot(c_ref[...], w_ref[W_DIFF + W_MOBA:, :], preferred_element_type=F32)
    o_ref[...] = x_ref[...] + acc


def _out_proj(x, o_a, o_b, o_c, w_out, layer, *, tm, tn):
    t, d = x.shape
    row = lambda width: pl.BlockSpec((tm, width), lambda i, j: (i, 0))
    return pl.pallas_call(
        _out_proj_kernel,
        grid=(t // tm, d // tn),
        in_specs=[pl.BlockSpec((tm, tn), lambda i, j: (i, j)), row(W_DIFF), row(W_MOBA), row(W_DIL),
                  pl.BlockSpec((None, MIX_WIDTH, tn), lambda i, j: (layer, 0, j))],
        out_specs=pl.BlockSpec((tm, tn), lambda i, j: (i, j)),
        out_shape=jax.ShapeDtypeStruct((t, d), F32),
        compiler_params=_params(("parallel", "parallel")),
        name="out_proj",
    )(x, o_a, o_b, o_c, w_out)


def _cross_attn_kernel(x_ref, g_ref, wq_ref, kv_ref, wo_ref, o_ref):
    x = x_ref[...]
    hn = _rms_norm(x, g_ref[...]).astype(BF16)
    q = jnp.dot(hn, wq_ref[...], preferred_element_type=F32).astype(BF16)
    kv = kv_ref[0]
    heads = []
    for h in range(N_MEM_HEADS):
        lo = h * HEAD_DIM
        s = lax.dot_general(q[:, lo:lo + HEAD_DIM], kv[:, lo:lo + HEAD_DIM], _NT,
                            preferred_element_type=F32)
        e = jnp.exp(s - jnp.max(s, axis=-1, keepdims=True))
        oh = jnp.dot(e.astype(BF16), kv[:, MEM_WIDTH + lo:MEM_WIDTH + lo + HEAD_DIM],
                     preferred_element_type=F32)
        heads.append((oh / jnp.sum(e, axis=-1, keepdims=True)).astype(BF16))
    o = jnp.concatenate(heads, axis=1)
    o_ref[...] = x + jnp.dot(o, wo_ref[...], preferred_element_type=F32)


def _cross_attn(x, g, w_cq, kv, w_co, layer, seq_len, *, tm):
    t, d = x.shape
    n_mem = kv.shape[1]
    tiles_per_batch = seq_len // tm
    return pl.pallas_call(
        _cross_attn_kernel,
        grid=(t // tm,),
        in_specs=[
            pl.BlockSpec((tm, d), lambda i: (i, 0)),
            pl.BlockSpec((None, 1, d), lambda i: (layer, 0, 0)),
            pl.BlockSpec((None, d, MEM_WIDTH), lambda i: (layer, 0, 0)),
            pl.BlockSpec((1, n_mem, 2 * MEM_WIDTH), lambda i: (i // tiles_per_batch, 0, 0)),
            pl.BlockSpec((None, MEM_WIDTH, d), lambda i: (layer, 0, 0)),
        ],
        out_specs=pl.BlockSpec((tm, d), lambda i: (i, 0)),
        out_shape=jax.ShapeDtypeStruct((t, d), F32),
        compiler_params=_params(("parallel",)),
        name="cross_attn",
    )(x, g, w_cq, kv, w_co)


def _ffn_kernel(x_ref, xp_ref, g_ref, wg_ref, wu_ref, cw_ref, cb_ref, wd_ref, gf_ref, o_ref, hn_ref,
                *, tm, tiles_per_batch, final_norm):
    i = pl.program_id(0)
    f = pl.program_id(1)

    @pl.when(f == 0)
    def _():
        g = g_ref[...]
        hn_ref[FFN_HALO:, :] = _rms_norm(x_ref[...], g).astype(BF16)
        prev = _rms_norm(xp_ref[...], g)
        prev = jnp.where(i % tiles_per_batch == 0, 0.0, prev)
        hn_ref[0:FFN_HALO, :] = prev.astype(BF16)
        o_ref[...] = x_ref[...]

    gate = jnp.dot(hn_ref[...], wg_ref[...], preferred_element_type=F32)
    up = jnp.dot(hn_ref[FFN_HALO:, :], wu_ref[...], preferred_element_type=F32)
    cw = cw_ref[...]
    conv = (cw[0:1] * gate[FFN_HALO - 2:FFN_HALO - 2 + tm]
            + cw[1:2] * gate[FFN_HALO - 1:FFN_HALO - 1 + tm]
            + cw[2:3] * gate[FFN_HALO:]) + cb_ref[...]
    act = conv * jax.nn.sigmoid(conv) * up
    o_ref[...] += jnp.dot(act.astype(BF16), wd_ref[...], preferred_element_type=F32)

    if final_norm:
        @pl.when(f == pl.num_programs(1) - 1)
        def _():
            o_ref[...] = _rms_norm(o_ref[...], gf_ref[...])


def _conv_ffn(x, g, w_up, conv_w, conv_b, w_down, g_final, layer, seq_len, *, tm, tf, final_norm):
    t, d = x.shape
    n_f = D_FF // tf
    tiles_per_batch = seq_len // tm
    halo_blocks = tm // FFN_HALO
    kern = functools.partial(_ffn_kernel, tm=tm, tiles_per_batch=tiles_per_batch, final_norm=final_norm)
    return pl.pallas_call(
        kern,
        grid=(t // tm, n_f),
        in_specs=[
            pl.BlockSpec((tm, d), lambda i, f: (i, 0)),
            pl.BlockSpec((FFN_HALO, d), lambda i, f: (jnp.maximum(i * halo_blocks - 1, 0), 0)),
            pl.BlockSpec((None, 1, d), lambda i, f: (layer, 0, 0)),
            pl.BlockSpec((None, d, tf), lambda i, f: (layer, 0, f)),
            pl.BlockSpec((None, d, tf), lambda i, f: (layer, 0, n_f + f)),
            pl.BlockSpec((None, CONV_WIDTH, tf), lambda i, f: (layer, 0, f)),
            pl.BlockSpec((None, 1, tf), lambda i, f: (layer, 0, f)),
            pl.BlockSpec((None, tf, d), lambda i, f: (layer, f, 0)),
            pl.BlockSpec((1, d), lambda i, f: (0, 0)),
        ],
        out_specs=pl.BlockSpec((tm, d), lambda i, f: (i, 0)),
        out_shape=jax.ShapeDtypeStruct((t, d), F32),
        scratch_shapes=[pltpu.VMEM((FFN_HALO + tm, d), BF16)],
        compiler_params=_params(("parallel", "arbitrary")),
        name="conv_ffn",
    )(x, x, g, w_up, w_up, conv_w, conv_b, w_down, g_final)


def _t5_bucket(dist):
    n = jnp.maximum(dist, 0)
    max_exact = N_BUCKETS // 2
    nf = jnp.maximum(n, 1).astype(F32)
    log_ratio = jnp.log(nf / max_exact) / math.log(REL_MAX_DIST / max_exact)
    large = max_exact + (log_ratio * (N_BUCKETS - max_exact)).astype(jnp.int32)
    large = jnp.minimum(large, N_BUCKETS - 1)
    return jnp.where(n < max_exact, n, large)


def _toeplitz(w, rows, cols, offset):
    length = cols + offset + 1
    assert w.shape[-1] == length and offset >= rows - 1
    lead = w.shape[:-1]
    flat = jnp.broadcast_to(w[..., None, :], lead + (rows, length)).reshape(lead + (rows * length,))
    shifted = flat[..., :rows * (length - 1)].reshape(lead + (rows, length - 1))
    return shifted[..., offset:offset + cols]


def _causal_bias_tiles(table, tile):
    n_tiles = -(-(REL_MAX_DIST + tile - 1) // tile) + 1
    dist = np.arange(-(tile - 1), n_tiles * tile + 1, dtype=np.int32)
    vec = jnp.where(dist[None, :] >= 0, table.astype(F32)[_t5_bucket(jnp.asarray(dist))].T, NEG_INF)
    tiles = _toeplitz(vec, tile, n_tiles * tile, tile - 1)
    tiles = tiles.reshape(-1, tile, n_tiles, tile).transpose(0, 2, 1, 3)
    return jnp.concatenate([tiles, jnp.full((tiles.shape[0], 1, tile, tile), NEG_INF, F32)], axis=1)


def _dilated_bias_tiles(table, window, dil):
    blk = DIL_BLOCK
    steps = window // dil
    j = np.arange(-(blk - 1), 2 * blk + 1, dtype=np.int32)
    valid = (j >= 0) & (j <= steps)
    bias = table.astype(F32)[_t5_bucket(jnp.asarray(np.maximum(j, 0) * dil))].T
    vec = jnp.where(valid[None, :], bias, NEG_INF)
    tiles = _toeplitz(vec, blk, 2 * blk, blk - 1)
    tiles = tiles.reshape(-1, blk, 2, blk).transpose(0, 2, 3, 1)
    masked = jnp.full((tiles.shape[0], 1, blk, blk), NEG_INF, F32)
    return jnp.concatenate([tiles[:, 1:2], tiles[:, 0:1], masked], axis=1)


def kernel(x, mem, w_in, w_out, g_mix, diff_lambda, diff_subln, rel_bias_table, g_cross, g_mem,
           w_cq, w_ckv, w_co, g_ffn, w_up, conv_w, conv_b, w_down, g_final):
    b, s, d = x.shape
    depth = w_in.shape[0]
    n_mem = mem.shape[1]
    t = b * s

    col_scale = np.ones((3 * MIX_WIDTH,), np.float32)
    col_scale[_COL_QA * HEAD_DIM:_COL_KA * HEAD_DIM] = DIFF_QK_DIM ** -0.5 * LOG2E
    col_scale[_COL_QB * HEAD_DIM:_COL_KB * HEAD_DIM] = HEAD_DIM ** -0.5 * LOG2E
    col_scale[_COL_QC * HEAD_DIM:(_COL_QC + N_HEADS_DIL) * HEAD_DIM] = HEAD_DIM ** -0.5
    w_in_b = (w_in * col_scale).astype(BF16)
    w_out_b = w_out.astype(BF16)
    w_cq_b = (w_cq * HEAD_DIM ** -0.5).astype(BF16)
    w_ckv_b = w_ckv.astype(BF16)
    w_co_b = w_co.astype(BF16)
    w_up_b = w_up.astype(BF16)
    w_down_b = w_down.astype(BF16)

    g_mix3 = g_mix.reshape(depth, 1, d)
    g_cross3 = g_cross.reshape(depth, 1, d)
    g_mem3 = g_mem.reshape(depth, 1, d)
    g_ffn3 = g_ffn.reshape(depth, 1, d)
    subln_col = diff_subln.reshape(depth, HEAD_DIM, 1)
    conv_b3 = conv_b.reshape(depth, 1, D_FF)
    g_final2 = g_final.reshape(1, d)

    n_ab = N_HEADS_DIFF + N_HEADS_MOBA
    bias_ab = _causal_bias_tiles(rel_bias_table[:, :n_ab] * LOG2E, ATTN_TILE)
    bias_c = jnp.stack([_dilated_bias_tiles(rel_bias_table[:, n_ab:], window, dil)
                        for window, dil in DIL_PAIRS])

    xf = x.reshape(t, d)
    memf = mem.reshape(b * n_mem, d)

    for layer in range(depth):
        qkv = _norm_matmul(xf, g_mix3, w_in_b, layer, tm=1024, tn=2048).reshape(b, s, 3 * MIX_WIDTH)
        o_a = _diff_attn(qkv, bias_ab, diff_lambda, subln_col, layer)
        o_b = _moba_attn(qkv, bias_ab)
        o_c = _dil_attn(qkv, bias_c)
        xf = _out_proj(xf, o_a.reshape(t, W_DIFF), o_b.reshape(t, W_MOBA), o_c.reshape(t, W_DIL),
                       w_out_b, layer, tm=1024, tn=1024)
        kv = _norm_matmul(memf, g_mem3, w_ckv_b, layer, tm=b * n_mem, tn=512)
        xf = _cross_attn(xf, g_cross3, w_cq_b, kv.reshape(b, n_mem, 2 * MEM_WIDTH), w_co_b, layer, s,
                         tm=512)
        xf = _conv_ffn(xf, g_ffn3, w_up_b, conv_w, conv_b3, w_down_b, g_final2, layer, s,
                       tm=512, tf=512, final_norm=(layer == depth - 1))
    return xf.reshape(b, s, d)
```

```python
import functools
import math

import numpy as np
import jax
import jax.numpy as jnp
from jax import lax
from jax.experimental import pallas as pl
from jax.experimental.pallas import tpu as pltpu

F32 = jnp.float32
BF16 = jnp.bfloat16

D_MODEL = 2048
HEAD_DIM = 128
N_HEADS_DIFF = 6
N_HEADS_MOBA = 4
N_HEADS_DIL = 6
W_DIFF = N_HEADS_DIFF * HEAD_DIM
W_MOBA = N_HEADS_MOBA * HEAD_DIM
W_DIL = N_HEADS_DIL * HEAD_DIM
MIX_WIDTH = W_DIFF + W_MOBA + W_DIL
DIFF_QK_DIM = HEAD_DIM // 2
MOBA_BLOCK = 256
MOBA_TOPK = 3
DIL_PAIRS = ((128, 1), (512, 4), (2048, 16))
DIL_BLOCK = 128
N_BUCKETS = 32
REL_MAX_DIST = 2048
N_MEM_HEADS = 4
MEM_WIDTH = N_MEM_HEADS * HEAD_DIM
D_FF = 5632
CONV_WIDTH = 3
NORM_EPS = 1e-6
NEG_INF = -1e30

_COL_QA = 0
_COL_KA = _COL_QA + N_HEADS_DIFF
_COL_VA = _COL_KA + N_HEADS_DIFF
_COL_QB = _COL_VA + N_HEADS_DIFF
_COL_KB = _COL_QB + N_HEADS_MOBA
_COL_VB = _COL_KB + N_HEADS_MOBA
_COL_QC = _COL_VB + N_HEADS_MOBA
_QKV_COLS = 3 * MIX_WIDTH // HEAD_DIM

ATTN_TILE = 256
Q_TILE = 1024
KV_TILE = 1024
VT_ROWS = HEAD_DIM + 16
LOG2E = math.log2(math.e)
DIL_BATCH = 8
DIL_PAD = DIL_BLOCK * max(dil for _, dil in DIL_PAIRS)
FFN_HALO = 16
VMEM_LIMIT = 56 * 1024 * 1024

_NT = (((1,), (1,)), ((), ()))


def _params(semantics):
    return pltpu.CompilerParams(dimension_semantics=semantics, vmem_limit_bytes=VMEM_LIMIT)


def _rms_norm(xf, g):
    y = xf * lax.rsqrt(jnp.mean(xf * xf, axis=-1, keepdims=True) + NORM_EPS)
    return y * g


def _norm_matmul_kernel(x_ref, g_ref, w_ref, o_ref, hn_ref):
    @pl.when(pl.program_id(1) == 0)
    def _():
        hn_ref[...] = _rms_norm(x_ref[...], g_ref[...]).astype(BF16)

    o_ref[...] = jnp.dot(hn_ref[...], w_ref[...], preferred_element_type=F32).astype(o_ref.dtype)


def _norm_matmul(x, g, w, layer, *, tm, tn):
    t, d = x.shape
    n = w.shape[-1]
    return pl.pallas_call(
        _norm_matmul_kernel,
        grid=(t // tm, n // tn),
        in_specs=[
            pl.BlockSpec((tm, d), lambda i, j: (i, 0)),
            pl.BlockSpec((None, 1, d), lambda i, j: (layer, 0, 0)),
            pl.BlockSpec((None, d, tn), lambda i, j: (layer, 0, j)),
        ],
        out_specs=pl.BlockSpec((tm, tn), lambda i, j: (i, j)),
        out_shape=jax.ShapeDtypeStruct((t, n), BF16),
        scratch_shapes=[pltpu.VMEM((tm, d), BF16)],
        compiler_params=_params(("parallel", "arbitrary")),
        name="norm_matmul",
    )(x, g, w)


def _bias_block(bias_ref, q_blk, k_blk, n_bias):
    cols = []
    for e in range(Q_TILE // ATTN_TILE):
        tiles = []
        for c in range(KV_TILE // ATTN_TILE):
            d = (q_blk + e) - (k_blk + c)
            tiles.append(bias_ref[0, jnp.where(d < 0, n_bias, jnp.minimum(d, n_bias - 1))])
        cols.append(jnp.concatenate(tiles, axis=0))
    return jnp.concatenate(cols, axis=1)


def _causal_sweep(i, logits, vt_ref, s_refs, m_ref, acc_ref):
    tk = KV_TILE
    n_steps = ((i + 1) * Q_TILE - 1) // tk + 1

    def key_start(j):
        return pl.multiple_of(jnp.minimum(j, n_steps - 1) * tk, tk)

    def logits_into(j, dst_ref):
        s_t = logits(j, key_start(j))
        dst_ref[0:tk, :] = s_t
        dst_ref[tk:tk + 1, :] = jnp.max(s_t, axis=0, keepdims=True)

    def step(j, slot):
        logits_into(j + 1, s_refs[1 - slot])
        m = m_ref[...]
        m_new = jnp.maximum(m, s_refs[slot][tk:tk + 1, :])
        p = jnp.exp2((s_refs[slot][0:tk, :] - m_new).astype(BF16))
        pv = jnp.dot(vt_ref[:, pl.ds(key_start(j), tk)], p, preferred_element_type=F32)
        acc_ref[...] = jnp.exp2(m - m_new) * acc_ref[...] + pv
        m_ref[...] = m_new

    m_ref[...] = jnp.full(m_ref.shape, NEG_INF, F32)
    acc_ref[...] = jnp.zeros(acc_ref.shape, F32)
    logits_into(0, s_refs[0])

    def body(j, carry):
        for slot in range(2):
            @pl.when(j % 2 == slot)
            def _(slot=slot):
                step(j, slot)
        return carry

    lax.fori_loop(0, n_steps, body, 0)
    return acc_ref[...]


def _sweep_scratch(seq_len, n):
    return [pltpu.VMEM((VT_ROWS, seq_len), BF16), pltpu.VMEM((KV_TILE + 8, n), F32),
            pltpu.VMEM((KV_TILE + 8, n), F32), pltpu.VMEM((1, n), F32), pltpu.VMEM((VT_ROWS, n), F32)]


def _transpose_values(v_ref, vt_ref, seq_len):
    def body(c, carry):
        start = pl.multiple_of(c * KV_TILE, KV_TILE)
        vt_ref[0:HEAD_DIM, pl.ds(start, KV_TILE)] = (
            v_ref[0, pl.ds(start, KV_TILE), :].astype(F32).T.astype(BF16))
        return carry
    lax.fori_loop(0, seq_len // KV_TILE, body, 0)
    vt_ref[HEAD_DIM:, :] = jnp.ones((VT_ROWS - HEAD_DIM, seq_len), BF16)


def _diff_attn_kernel(q_ref, k_ref, v_ref, bias_ref, lam_ref, sg_ref, o_ref, vt_ref, sa_ref, sb_ref,
                      m_ref, acc_ref, *, n_bias, lam_init, seq_len):
    tq, tk = Q_TILE, KV_TILE
    i = pl.program_id(2)

    @pl.when(i == 0)
    def _():
        _transpose_values(v_ref, vt_ref, seq_len)

    q_t = q_ref[0].astype(F32).T
    row = lax.broadcasted_iota(jnp.int32, q_t.shape, 0)
    qm_t = jnp.concatenate([jnp.where(row < DIFF_QK_DIM, q_t, 0.0),
                            jnp.where(row >= DIFF_QK_DIM, q_t, 0.0)], axis=1).astype(BF16)

    def logits(j, start):
        s_t = jnp.dot(k_ref[0, pl.ds(start, tk), :], qm_t, preferred_element_type=F32)
        b_t = _bias_block(bias_ref, i * (tq // ATTN_TILE), j * (tk // ATTN_TILE), n_bias)
        return s_t + jnp.concatenate([b_t, b_t], axis=1)

    acc = _causal_sweep(i, logits, vt_ref, (sa_ref, sb_ref), m_ref, acc_ref)
    o_t = acc[:HEAD_DIM] / acc[HEAD_DIM:HEAD_DIM + 1]
    lp = lam_ref[...]
    lam = (jnp.exp(jnp.sum(lp[0:1] * lp[1:2], keepdims=True))
           - jnp.exp(jnp.sum(lp[2:3] * lp[3:4], keepdims=True)) + lam_init)
    od_t = o_t[:, :tq] - lam * o_t[:, tq:]
    y_t = od_t * lax.rsqrt(jnp.mean(od_t * od_t, axis=0, keepdims=True) + NORM_EPS) * sg_ref[...]
    o_ref[0] = (y_t * (1.0 - lam_init)).T.astype(o_ref.dtype)


def _diff_attn(qkv, bias_t, diff_lambda, diff_subln_col, layer):
    b, s, _ = qkv.shape
    n_bias = bias_t.shape[1] - 1
    assert s % KV_TILE == 0 and s % Q_TILE == 0
    lam_init = 0.8 - 0.6 * math.exp(-0.3 * layer)
    kern = functools.partial(_diff_attn_kernel, n_bias=n_bias, lam_init=lam_init, seq_len=s)
    return pl.pallas_call(
        kern,
        grid=(b, N_HEADS_DIFF, s // Q_TILE),
        in_specs=[
            pl.BlockSpec((1, Q_TILE, HEAD_DIM), lambda bi, h, i: (bi, i, _COL_QA + h)),
            pl.BlockSpec((1, s, HEAD_DIM), lambda bi, h, i: (bi, 0, _COL_KA + h)),
            pl.BlockSpec((1, s, HEAD_DIM), lambda bi, h, i: (bi, 0, _COL_VA + h)),
            pl.BlockSpec((1, n_bias + 1, ATTN_TILE, ATTN_TILE), lambda bi, h, i: (h, 0, 0, 0)),
            pl.BlockSpec((None, 4, DIFF_QK_DIM), lambda bi, h, i: (layer, 0, 0)),
            pl.BlockSpec((None, HEAD_DIM, 1), lambda bi, h, i: (layer, 0, 0)),
        ],
        out_specs=pl.BlockSpec((1, Q_TILE, HEAD_DIM), lambda bi, h, i: (bi, i, h)),
        out_shape=jax.ShapeDtypeStruct((b, s, W_DIFF), BF16),
        scratch_shapes=_sweep_scratch(s, 2 * Q_TILE),
        compiler_params=_params(("parallel", "parallel", "arbitrary")),
        name="diff_attn",
    )(qkv, qkv, qkv, bias_t, diff_lambda, diff_subln_col)


def _moba_kernel(q_ref, k_ref, v_ref, lanes_ref, bias_ref, o_ref, vt_ref, sa_ref, sb_ref, m_ref, acc_ref,
                 kmean_ref, *, n_bias, n_blocks, seq_len):
    tq, tk = Q_TILE, KV_TILE
    i = pl.program_id(2)

    @pl.when(i == 0)
    def _():
        _transpose_values(v_ref, vt_ref, seq_len)
        kf = k_ref[0].astype(F32).reshape(n_blocks, MOBA_BLOCK, HEAD_DIM)
        kmean_ref[...] = jnp.mean(kf, axis=1)

    q_t = q_ref[0].astype(F32).T
    gate = jnp.dot(kmean_ref[...], q_t, preferred_element_type=F32,
                   precision=lax.Precision.HIGHEST)
    blk = lax.broadcasted_iota(jnp.int32, gate.shape, 0)
    lane = lax.broadcasted_iota(jnp.int32, gate.shape, 1)
    own = i * (tq // MOBA_BLOCK)
    for e in range(1, tq // MOBA_BLOCK):
        own = own + (lane >= e * MOBA_BLOCK).astype(jnp.int32)
    past = blk < own
    gm = jnp.where(past, gate, NEG_INF)
    rank = jnp.zeros(gate.shape, jnp.int32)
    for jp in range(n_blocks):
        gj = gm[jp:jp + 1, :]
        beats = (gj > gm) | ((gj == gm) & (jp < blk))
        rank = rank + beats.astype(jnp.int32)
    allowed = (past & (rank < MOBA_TOPK)) | (blk == own)
    mask_t = jnp.where(allowed, 0.0, NEG_INF)
    mask_t = jnp.concatenate([mask_t, jnp.zeros((HEAD_DIM - n_blocks, tq), F32)], axis=0)
    q_aug_t = jnp.concatenate([q_t.astype(BF16), mask_t.astype(BF16)], axis=0)

    def logits(j, start):
        k_aug = jnp.concatenate([k_ref[0, pl.ds(start, tk), :], lanes_ref[pl.ds(start, tk), :]], axis=1)
        s_t = jnp.dot(k_aug, q_aug_t, preferred_element_type=F32)
        return s_t + _bias_block(bias_ref, i * (tq // ATTN_TILE), j * (tk // ATTN_TILE), n_bias)

    acc = _causal_sweep(i, logits, vt_ref, (sa_ref, sb_ref), m_ref, acc_ref)
    o_ref[0] = (acc[:HEAD_DIM] / acc[HEAD_DIM:HEAD_DIM + 1]).T.astype(o_ref.dtype)


def _moba_attn(qkv, bias_t):
    b, s, _ = qkv.shape
    n_bias = bias_t.shape[1] - 1
    n_blocks = s // MOBA_BLOCK
    assert s % KV_TILE == 0 and s % Q_TILE == 0 and n_blocks <= HEAD_DIM and n_blocks % 8 == 0
    block_lanes = np.zeros((s, HEAD_DIM), np.float32)
    block_lanes[np.arange(s), np.arange(s) // MOBA_BLOCK] = 1.0
    kern = functools.partial(_moba_kernel, n_bias=n_bias, n_blocks=n_blocks, seq_len=s)
    return pl.pallas_call(
        kern,
        grid=(b, N_HEADS_MOBA, s // Q_TILE),
        in_specs=[
            pl.BlockSpec((1, Q_TILE, HEAD_DIM), lambda bi, h, i: (bi, i, _COL_QB + h)),
            pl.BlockSpec((1, s, HEAD_DIM), lambda bi, h, i: (bi, 0, _COL_KB + h)),
            pl.BlockSpec((1, s, HEAD_DIM), lambda bi, h, i: (bi, 0, _COL_VB + h)),
            pl.BlockSpec((s, HEAD_DIM), lambda bi, h, i: (0, 0)),
            pl.BlockSpec((1, n_bias + 1, ATTN_TILE, ATTN_TILE), lambda bi, h, i: (N_HEADS_DIFF + h, 0, 0, 0)),
        ],
        out_specs=pl.BlockSpec((1, Q_TILE, HEAD_DIM), lambda bi, h, i: (bi, i, h)),
        out_shape=jax.ShapeDtypeStruct((b, s, W_MOBA), BF16),
        scratch_shapes=_sweep_scratch(s, Q_TILE) + [pltpu.VMEM((n_blocks, HEAD_DIM), F32)],
        compiler_params=_params(("parallel", "parallel", "arbitrary")),
        name="moba_attn",
    )(qkv, qkv, qkv, jnp.asarray(block_lanes, BF16), bias_t)


def _dil_batch(seq_len, dil):
    return min(DIL_BATCH, seq_len // dil // DIL_BLOCK)


def _dil_kernel(q_ref, k_ref, v_ref, bias_ref, o_ref, qf_ref, kf_ref, vf_ref, m_ref, l_ref, acc_ref,
                *, seq_len):
    blk = DIL_BLOCK
    chunk = KV_TILE

    kf_ref[0:DIL_PAD, :] = jnp.zeros((DIL_PAD, HEAD_DIM), F32)
    vf_ref[0:DIL_PAD, :] = jnp.zeros((DIL_PAD, HEAD_DIM), F32)

    def widen(c, carry):
        src = pl.ds(pl.multiple_of(c * chunk, chunk), chunk)
        dst = pl.ds(pl.multiple_of(DIL_PAD + c * chunk, chunk), chunk)
        qf_ref[src, :] = q_ref[0, src, :].astype(F32)
        kf_ref[dst, :] = k_ref[0, src, :].astype(F32)
        vf_ref[dst, :] = v_ref[0, src, :].astype(F32)
        return carry
    lax.fori_loop(0, seq_len // chunk, widen, 0)

    for br, (_, dil) in enumerate(DIL_PAIRS):
        nb = _dil_batch(seq_len, dil)
        rows = nb * blk
        trips_per_res = seq_len // dil // rows

        def rows_of(start, n, dil=dil):
            return pl.ds(start, n, stride=dil) if dil > 1 else pl.ds(start, n)

        def trip(idx, carry, br=br, dil=dil, nb=nb, rows=rows, trips_per_res=trips_per_res, rows_of=rows_of):
            res = idx // trips_per_res
            t = idx % trips_per_res
            tok = res + t * rows * dil
            q = qf_ref[rows_of(tok, rows), :].astype(BF16).reshape(nb, blk, HEAD_DIM)
            kk = kf_ref[rows_of(DIL_PAD + tok - blk * dil, rows + blk), :].astype(BF16)
            vv = vf_ref[rows_of(DIL_PAD + tok - blk * dil, rows + blk), :].astype(BF16)
            k_prev, k_cur = kk[:rows].reshape(nb, blk, HEAD_DIM), kk[blk:].reshape(nb, blk, HEAD_DIM)
            v_prev, v_cur = vv[:rows].reshape(nb, blk, HEAD_DIM), vv[blk:].reshape(nb, blk, HEAD_DIM)
            b_first = bias_ref[br, 0, jnp.where(t == 0, 2, 0)]
            b_prev = jnp.concatenate(
                [b_first[None], jnp.broadcast_to(bias_ref[br, 0, 0][None], (nb - 1, blk, blk))], axis=0)
            s_prev = jnp.einsum("nqd,nkd->nqk", q, k_prev, preferred_element_type=F32) + b_prev
            s_cur = jnp.einsum("nqd,nkd->nqk", q, k_cur, preferred_element_type=F32) + bias_ref[br, 0, 1][None]
            m_b = jnp.maximum(jnp.max(s_prev, axis=-1, keepdims=True), jnp.max(s_cur, axis=-1, keepdims=True))
            e_prev = jnp.exp(s_prev - m_b)
            e_cur = jnp.exp(s_cur - m_b)
            l_b = jnp.sum(e_prev, axis=-1, keepdims=True) + jnp.sum(e_cur, axis=-1, keepdims=True)
            o_b = (jnp.einsum("nqk,nkd->nqd", e_prev.astype(BF16), v_prev, preferred_element_type=F32)
                   + jnp.einsum("nqk,nkd->nqd", e_cur.astype(BF16), v_cur, preferred_element_type=F32))
            m_b = jnp.broadcast_to(m_b.reshape(rows, 1), (rows, HEAD_DIM))
            l_b = jnp.broadcast_to(l_b.reshape(rows, 1), (rows, HEAD_DIM))
            o_b = o_b.reshape(rows, HEAD_DIM)
            dst = rows_of(tok, rows)
            if br == 0:
                m_ref[dst, :], l_ref[dst, :], acc_ref[dst, :] = m_b, l_b, o_b
            else:
                m_old = m_ref[dst, :]
                m_new = jnp.maximum(m_old, m_b)
                w_old = jnp.exp(m_old - m_new)
                w_b = jnp.exp(m_b - m_new)
                m_ref[dst, :] = m_new
                l_ref[dst, :] = w_old * l_ref[dst, :] + w_b * l_b
                acc_ref[dst, :] = w_old * acc_ref[dst, :] + w_b * o_b
            return carry
        lax.fori_loop(0, dil * trips_per_res, trip, 0)

    def finish(c, carry):
        src = pl.ds(pl.multiple_of(c * chunk, chunk), chunk)
        o_ref[0, src, :] = (acc_ref[src, :] / l_ref[src, :]).astype(o_ref.dtype)
        return carry
    lax.fori_loop(0, seq_len // chunk, finish, 0)


def _dil_attn(qkv, bias):
    b, s, _ = qkv.shape
    n_br = len(DIL_PAIRS)
    assert all(s % (dil * _dil_batch(s, dil) * DIL_BLOCK) == 0 for _, dil in DIL_PAIRS) and s % KV_TILE == 0
    slab = lambda col: pl.BlockSpec((1, s, HEAD_DIM), lambda bi, h: (bi, 0, col + h))
    return pl.pallas_call(
        functools.partial(_dil_kernel, seq_len=s),
        grid=(b, N_HEADS_DIL),
        in_specs=[slab(_COL_QC), slab(_COL_QC + N_HEADS_DIL), slab(_COL_QC + 2 * N_HEADS_DIL),
                  pl.BlockSpec((n_br, 1, 3, DIL_BLOCK, DIL_BLOCK), lambda bi, h: (0, h, 0, 0, 0))],
        out_specs=pl.BlockSpec((1, s, HEAD_DIM), lambda bi, h: (bi, 0, h)),
        out_shape=jax.ShapeDtypeStruct((b, s, W_DIL), BF16),
        scratch_shapes=[pltpu.VMEM((s, HEAD_DIM), F32), pltpu.VMEM((DIL_PAD + s, HEAD_DIM), F32),
                        pltpu.VMEM((DIL_PAD + s, HEAD_DIM), F32), pltpu.VMEM((s, HEAD_DIM), F32),
                        pltpu.VMEM((s, HEAD_DIM), F32), pltpu.VMEM((s, HEAD_DIM), F32)],
        compiler_params=_params(("parallel", "parallel")),
        name="dil_attn",
    )(qkv, qkv, qkv, bias)


def _out_proj_kernel(x_ref, a_ref, b_ref, c_ref, w_ref, o_ref):
    acc = jnp.dot(a_ref[...], w_ref[0:W_DIFF, :], preferred_element_type=F32)
    acc += jnp.dot(b_ref[...], w_ref[W_DIFF:W_DIFF + W_MOBA, :], preferred_element_type=F32)
    acc += jnp.dot(c_ref[...], w_ref[W_DIFF + W_MOBA:, :], preferred_element_type=F32)
    o_ref[...] = x_ref[...] + acc


def _out_proj(x, o_a, o_b, o_c, w_out, layer, *, tm, tn):
    t, d = x.shape
    row = lambda width: pl.BlockSpec((tm, width), lambda i, j: (i, 0))
    return pl.pallas_call(
        _out_proj_kernel,
        grid=(t // tm, d // tn),
        in_specs=[pl.BlockSpec((tm, tn), lambda i, j: (i, j)), row(W_DIFF), row(W_MOBA), row(W_DIL),
                  pl.BlockSpec((None, MIX_WIDTH, tn), lambda i, j: (layer, 0, j))],
        out_specs=pl.BlockSpec((tm, tn), lambda i, j: (i, j)),
        out_shape=jax.ShapeDtypeStruct((t, d), F32),
        compiler_params=_params(("parallel", "parallel")),
        name="out_proj",
    )(x, o_a, o_b, o_c, w_out)


def _cross_attn_kernel(x_ref, g_ref, wq_ref, kv_ref, wo_ref, o_ref):
    x = x_ref[...]
    hn = _rms_norm(x, g_ref[...]).astype(BF16)
    q = jnp.dot(hn, wq_ref[...], preferred_element_type=F32).astype(BF16)
    kv = kv_ref[0]
    heads = []
    for h in range(N_MEM_HEADS):
        lo = h * HEAD_DIM
        s = lax.dot_general(q[:, lo:lo + HEAD_DIM], kv[:, lo:lo + HEAD_DIM], _NT,
                            preferred_element_type=F32)
        e = jnp.exp(s - jnp.max(s, axis=-1, keepdims=True))
        oh = jnp.dot(e.astype(BF16), kv[:, MEM_WIDTH + lo:MEM_WIDTH + lo + HEAD_DIM],
                     preferred_element_type=F32)
        heads.append((oh / jnp.sum(e, axis=-1, keepdims=True)).astype(BF16))
    o = jnp.concatenate(heads, axis=1)
    o_ref[...] = x + jnp.dot(o, wo_ref[...], preferred_element_type=F32)


def _cross_attn(x, g, w_cq, kv, w_co, layer, seq_len, *, tm):
    t, d = x.shape
    n_mem = kv.shape[1]
    tiles_per_batch = seq_len // tm
    return pl.pallas_call(
        _cross_attn_kernel,
        grid=(t // tm,),
        in_specs=[
            pl.BlockSpec((tm, d), lambda i: (i, 0)),
            pl.BlockSpec((None, 1, d), lambda i: (layer, 0, 0)),
            pl.BlockSpec((None, d, MEM_WIDTH), lambda i: (layer, 0, 0)),
            pl.BlockSpec((1, n_mem, 2 * MEM_WIDTH), lambda i: (i // tiles_per_batch, 0, 0)),
            pl.BlockSpec((None, MEM_WIDTH, d), lambda i: (layer, 0, 0)),
        ],
        out_specs=pl.BlockSpec((tm, d), lambda i: (i, 0)),
        out_shape=jax.ShapeDtypeStruct((t, d), F32),
        compiler_params=_params(("parallel",)),
        name="cross_attn",
    )(x, g, w_cq, kv, w_co)


def _ffn_kernel(x_ref, xp_ref, g_ref, wg_ref, wu_ref, cw_ref, cb_ref, wd_ref, gf_ref, o_ref, hn_ref,
                *, tm, tiles_per_batch, final_norm):
    i = pl.program_id(0)
    f = pl.program_id(1)

    @pl.when(f == 0)
    def _():
        g = g_ref[...]
        hn_ref[FFN_HALO:, :] = _rms_norm(x_ref[...], g).astype(BF16)
        prev = _rms_norm(xp_ref[...], g)
        prev = jnp.where(i % tiles_per_batch == 0, 0.0, prev)
        hn_ref[0:FFN_HALO, :] = prev.astype(BF16)
        o_ref[...] = x_ref[...]

    gate = jnp.dot(hn_ref[...], wg_ref[...], preferred_element_type=F32)
    up = jnp.dot(hn_ref[FFN_HALO:, :], wu_ref[...], preferred_element_type=F32)
    cw = cw_ref[...]
    conv = (cw[0:1] * gate[FFN_HALO - 2:FFN_HALO - 2 + tm]
            + cw[1:2] * gate[FFN_HALO - 1:FFN_HALO - 1 + tm]
            + cw[2:3] * gate[FFN_HALO:]) + cb_ref[...]
    act = conv * jax.nn.sigmoid(conv) * up
    o_ref[...] += jnp.dot(act.astype(BF16), wd_ref[...], preferred_element_type=F32)

    if final_norm:
        @pl.when(f == pl.num_programs(1) - 1)
        def _():
            o_ref[...] = _rms_norm(o_ref[...], gf_ref[...])


def _conv_ffn(x, g, w_up, conv_w, conv_b, w_down, g_final, layer, seq_len, *, tm, tf, final_norm):
    t, d = x.shape
    n_f = D_FF // tf
    tiles_per_batch = seq_len // tm
    halo_blocks = tm // FFN_HALO
    kern = functools.partial(_ffn_kernel, tm=tm, tiles_per_batch=tiles_per_batch, final_norm=final_norm)
    return pl.pallas_call(
        kern,
        grid=(t // tm, n_f),
        in_specs=[
            pl.BlockSpec((tm, d), lambda i, f: (i, 0)),
            pl.BlockSpec((FFN_HALO, d), lambda i, f: (jnp.maximum(i * halo_blocks - 1, 0), 0)),
            pl.BlockSpec((None, 1, d), lambda i, f: (layer, 0, 0)),
            pl.BlockSpec((None, d, tf), lambda i, f: (layer, 0, f)),
            pl.BlockSpec((None, d, tf), lambda i, f: (layer, 0, n_f + f)),
            pl.BlockSpec((None, CONV_WIDTH, tf), lambda i, f: (layer, 0, f)),
            pl.BlockSpec((None, 1, tf), lambda i, f: (layer, 0, f)),
            pl.BlockSpec((None, tf, d), lambda i, f: (layer, f, 0)),
            pl.BlockSpec((1, d), lambda i, f: (0, 0)),
        ],
        out_specs=pl.BlockSpec((tm, d), lambda i, f: (i, 0)),
        out_shape=jax.ShapeDtypeStruct((t, d), F32),
        scratch_shapes=[pltpu.VMEM((FFN_HALO + tm, d), BF16)],
        compiler_params=_params(("parallel", "arbitrary")),
        name="conv_ffn",
    )(x, x, g, w_up, w_up, conv_w, conv_b, w_down, g_final)


def _t5_bucket(dist):
    n = jnp.maximum(dist, 0)
    max_exact = N_BUCKETS // 2
    nf = jnp.maximum(n, 1).astype(F32)
    log_ratio = jnp.log(nf / max_exact) / math.log(REL_MAX_DIST / max_exact)
    large = max_exact + (log_ratio * (N_BUCKETS - max_exact)).astype(jnp.int32)
    large = jnp.minimum(large, N_BUCKETS - 1)
    return jnp.where(n < max_exact, n, large)


def _toeplitz(w, rows, cols, offset):
    length = cols + offset + 1
    assert w.shape[-1] == length and offset >= rows - 1
    lead = w.shape[:-1]
    flat = jnp.broadcast_to(w[..., None, :], lead + (rows, length)).reshape(lead + (rows * length,))
    shifted = flat[..., :rows * (length - 1)].reshape(lead + (rows, length - 1))
    return shifted[..., offset:offset + cols]


def _causal_bias_tiles(table, tile):
    n_tiles = -(-(REL_MAX_DIST + tile - 1) // tile) + 1
    dist = np.arange(-(tile - 1), n_tiles * tile + 1, dtype=np.int32)
    vec = jnp.where(dist[None, :] >= 0, table.astype(F32)[_t5_bucket(jnp.asarray(dist))].T, NEG_INF)
    tiles = _toeplitz(vec, tile, n_tiles * tile, tile - 1)
    tiles = tiles.reshape(-1, tile, n_tiles, tile).transpose(0, 2, 1, 3)
    return jnp.concatenate([tiles, jnp.full((tiles.shape[0], 1, tile, tile), NEG_INF, F32)], axis=1)


def _dilated_bias_tiles(table, window, dil):
    blk = DIL_BLOCK
    steps = window // dil
    j = np.arange(-(blk - 1), 2 * blk + 1, dtype=np.int32)
    valid = (j >= 0) & (j <= steps)
    bias = table.astype(F32)[_t5_bucket(jnp.asarray(np.maximum(j, 0) * dil))].T
    vec = jnp.where(valid[None, :], bias, NEG_INF)
    tiles = _toeplitz(vec, blk, 2 * blk, blk - 1)
    tiles = tiles.reshape(-1, blk, 2, blk).transpose(0, 2, 3, 1)
    masked = jnp.full((tiles.shape[0], 1, blk, blk), NEG_INF, F32)
    return jnp.concatenate([tiles[:, 1:2], tiles[:, 0:1], masked], axis=1)


def kernel(x, mem, w_in, w_out, g_mix, diff_lambda, diff_subln, rel_bias_table, g_cross, g_mem,
           w_cq, w_ckv, w_co, g_ffn, w_up, conv_w, conv_b, w_down, g_final):
    b, s, d = x.shape
    depth = w_in.shape[0]
    n_mem = mem.shape[1]
    t = b * s

    col_scale = np.ones((3 * MIX_WIDTH,), np.float32)
    col_scale[_COL_QA * HEAD_DIM:_COL_KA * HEAD_DIM] = DIFF_QK_DIM ** -0.5 * LOG2E
    col_scale[_COL_QB * HEAD_DIM:_COL_KB * HEAD_DIM] = HEAD_DIM ** -0.5 * LOG2E
    col_scale[_COL_QC * HEAD_DIM:(_COL_QC + N_HEADS_DIL) * HEAD_DIM] = HEAD_DIM ** -0.5
    w_in_b = (w_in * col_scale).astype(BF16)
    w_out_b = w_out.astype(BF16)
    w_cq_b = (w_cq * HEAD_DIM ** -0.5).astype(BF16)
    w_ckv_b = w_ckv.astype(BF16)
    w_co_b = w_co.astype(BF16)
    w_up_b = w_up.astype(BF16)
    w_down_b = w_down.astype(BF16)

    g_mix3 = g_mix.reshape(depth, 1, d)
    g_cross3 = g_cross.reshape(depth, 1, d)
    g_mem3 = g_mem.reshape(depth, 1, d)
    g_ffn3 = g_ffn.reshape(depth, 1, d)
    subln_col = diff_subln.reshape(depth, HEAD_DIM, 1)
    conv_b3 = conv_b.reshape(depth, 1, D_FF)
    g_final2 = g_final.reshape(1, d)

    n_ab = N_HEADS_DIFF + N_HEADS_MOBA
    bias_ab = _causal_bias_tiles(rel_bias_table[:, :n_ab] * LOG2E, ATTN_TILE)
    bias_c = jnp.stack([_dilated_bias_tiles(rel_bias_table[:, n_ab:], window, dil)
                        for window, dil in DIL_PAIRS])

    xf = x.reshape(t, d)
    memf = mem.reshape(b * n_mem, d)

    for layer in range(depth):
        qkv = _norm_matmul(xf, g_mix3, w_in_b, layer, tm=1024, tn=2048).reshape(b, s, 3 * MIX_WIDTH)
        o_a = _diff_attn(qkv, bias_ab, diff_lambda, subln_col, layer)
        o_b = _moba_attn(qkv, bias_ab)
        o_c = _dil_attn(qkv, bias_c)
        xf = _out_proj(xf, o_a.reshape(t, W_DIFF), o_b.reshape(t, W_MOBA), o_c.reshape(t, W_DIL),
                       w_out_b, layer, tm=1024, tn=1024)
        kv = _norm_matmul(memf, g_mem3, w_ckv_b, layer, tm=b * n_mem, tn=512)
        xf = _cross_attn(xf, g_cross3, w_cq_b, kv.reshape(b, n_mem, 2 * MEM_WIDTH), w_co_b, layer, s,
                         tm=512)
        xf = _conv_ffn(xf, g_ffn3, w_up_b, conv_w, conv_b3, w_down_b, g_final2, layer, s,
                       tm=512, tf=512, final_norm=(layer == depth - 1))
    return xf.reshape(b, s, d)
```

```python
import functools
import math

import numpy as np
import jax
import jax.numpy as jnp
from jax import lax
from jax.experimental import pallas as pl
from jax.experimental.pallas import tpu as pltpu

F32 = jnp.float32
BF16 = jnp.bfloat16

HEAD_DIM = 128
N_HEADS_DIFF = 6
N_HEADS_MOBA = 4
N_HEADS_DIL = 6
W_DIFF = N_HEADS_DIFF * HEAD_DIM
W_MOBA = N_HEADS_MOBA * HEAD_DIM
W_DIL = N_HEADS_DIL * HEAD_DIM
MIX_WIDTH = W_DIFF + W_MOBA + W_DIL
DIFF_QK_DIM = HEAD_DIM // 2
MOBA_BLOCK = 256
MOBA_TOPK = 3
DIL_PAIRS = ((128, 1), (512, 4), (2048, 16))
DIL_BLOCK = 128
N_BUCKETS = 32
REL_MAX_DIST = 2048
N_MEM_HEADS = 4
MEM_WIDTH = N_MEM_HEADS * HEAD_DIM
D_FF = 5632
CONV_WIDTH = 3
NORM_EPS = 1e-6
NEG_INF = -1e30

_COL_QA = 0
_COL_KA = _COL_QA + N_HEADS_DIFF
_COL_VA = _COL_KA + N_HEADS_DIFF
_COL_QB = _COL_VA + N_HEADS_DIFF
_COL_KB = _COL_QB + N_HEADS_MOBA
_COL_VB = _COL_KB + N_HEADS_MOBA
_COL_QC = _COL_VB + N_HEADS_MOBA

ATTN_TILE = 256
Q_TILE = 1024
KV_TILE = 1024
VT_ROWS = HEAD_DIM + 16
LOG2E = math.log2(math.e)
DIL_BATCH = 8
DIL_PAD = DIL_BLOCK * max(dil for _, dil in DIL_PAIRS)
FFN_HALO = 16
VMEM_LIMIT = 56 * 1024 * 1024

_NT = (((1,), (1,)), ((), ()))


def _params(semantics):
    return pltpu.CompilerParams(dimension_semantics=semantics, vmem_limit_bytes=VMEM_LIMIT)


def _rms_norm(xf, g):
    y = xf * lax.rsqrt(jnp.mean(xf * xf, axis=-1, keepdims=True) + NORM_EPS)
    return y * g


def _norm_matmul_kernel(x_ref, g_ref, w_ref, o_ref, hn_ref):
    @pl.when(pl.program_id(1) == 0)
    def _():
        hn_ref[...] = _rms_norm(x_ref[...], g_ref[...]).astype(BF16)

    o_ref[...] = jnp.dot(hn_ref[...], w_ref[...], preferred_element_type=F32).astype(o_ref.dtype)


def _norm_matmul(x, g, w, layer, *, tm, tn):
    t, d = x.shape
    n = w.shape[-1]
    return pl.pallas_call(
        _norm_matmul_kernel,
        grid=(t // tm, n // tn),
        in_specs=[
            pl.BlockSpec((tm, d), lambda i, j: (i, 0)),
            pl.BlockSpec((None, 1, d), lambda i, j: (layer, 0, 0)),
            pl.BlockSpec((None, d, tn), lambda i, j: (layer, 0, j)),
        ],
        out_specs=pl.BlockSpec((tm, tn), lambda i, j: (i, j)),
        out_shape=jax.ShapeDtypeStruct((t, n), BF16),
        scratch_shapes=[pltpu.VMEM((tm, d), BF16)],
        compiler_params=_params(("parallel", "arbitrary")),
        name="norm_matmul",
    )(x, g, w)


def _bias_block(bias_ref, q_blk, k_blk, n_bias):
    cols = []
    for e in range(Q_TILE // ATTN_TILE):
        tiles = []
        for c in range(KV_TILE // ATTN_TILE):
            d = (q_blk + e) - (k_blk + c)
            tiles.append(bias_ref[0, jnp.where(d < 0, n_bias, jnp.minimum(d, n_bias - 1))])
        cols.append(jnp.concatenate(tiles, axis=0))
    return jnp.concatenate(cols, axis=1)


def _causal_sweep(i, logits, vt_ref, s_refs, m_ref, acc_ref):
    tk = KV_TILE
    n_steps = ((i + 1) * Q_TILE - 1) // tk + 1

    def key_start(j):
        return pl.multiple_of(j * tk, tk)

    def logits_into(j, dst_ref):
        s_t = logits(j, key_start(j))
        dst_ref[0:tk, :] = s_t
        dst_ref[tk:tk + 1, :] = jnp.max(s_t, axis=0, keepdims=True)

    def step(j, slot, prefetch):
        if prefetch:
            logits_into(j + 1, s_refs[1 - slot])
        m = m_ref[...]
        m_new = jnp.maximum(m, s_refs[slot][tk:tk + 1, :])
        p = jnp.exp2((s_refs[slot][0:tk, :] - m_new).astype(BF16))
        pv = jnp.dot(vt_ref[:, pl.ds(key_start(j), tk)], p, preferred_element_type=F32)
        acc_ref[...] = jnp.exp2(m - m_new) * acc_ref[...] + pv
        m_ref[...] = m_new

    m_ref[...] = jnp.full(m_ref.shape, NEG_INF, F32)
    acc_ref[...] = jnp.zeros(acc_ref.shape, F32)
    logits_into(0, s_refs[0])

    def body(j, carry):
        for slot in range(2):
            @pl.when(j % 2 == slot)
            def _(slot=slot):
                step(j, slot, True)
        return carry

    lax.fori_loop(0, n_steps - 1, body, 0)
    for slot in range(2):
        @pl.when((n_steps - 1) % 2 == slot)
        def _(slot=slot):
            step(n_steps - 1, slot, False)
    return acc_ref[...]


def _sweep_scratch(seq_len, n):
    return [pltpu.VMEM((VT_ROWS, seq_len), BF16), pltpu.VMEM((KV_TILE + 8, n), F32),
            pltpu.VMEM((KV_TILE + 8, n), F32), pltpu.VMEM((1, n), F32), pltpu.VMEM((VT_ROWS, n), F32)]


def _transpose_values(v_ref, vt_ref, seq_len):
    def body(c, carry):
        start = pl.multiple_of(c * KV_TILE, KV_TILE)
        vt_ref[0:HEAD_DIM, pl.ds(start, KV_TILE)] = (
            v_ref[0, pl.ds(start, KV_TILE), :].astype(F32).T.astype(BF16))
        return carry
    lax.fori_loop(0, seq_len // KV_TILE, body, 0)
    vt_ref[HEAD_DIM:, :] = jnp.ones((VT_ROWS - HEAD_DIM, seq_len), BF16)


def _diff_attn_kernel(q_ref, k_ref, v_ref, bias_ref, lam_ref, sg_ref, o_ref, vt_ref, sa_ref, sb_ref,
                      m_ref, acc_ref, *, n_bias, lam_init, seq_len):
    tq, tk = Q_TILE, KV_TILE
    i = pl.program_id(2)

    @pl.when(i == 0)
    def _():
        _transpose_values(v_ref, vt_ref, seq_len)

    q_t = q_ref[0].astype(F32).T
    row = lax.broadcasted_iota(jnp.int32, q_t.shape, 0)
    qm_t = jnp.concatenate([jnp.where(row < DIFF_QK_DIM, q_t, 0.0),
                            jnp.where(row >= DIFF_QK_DIM, q_t, 0.0)], axis=1).astype(BF16)

    def logits(j, start):
        s_t = jnp.dot(k_ref[0, pl.ds(start, tk), :], qm_t, preferred_element_type=F32)
        b_t = _bias_block(bias_ref, i * (tq // ATTN_TILE), j * (tk // ATTN_TILE), n_bias)
        return s_t + jnp.concatenate([b_t, b_t], axis=1)

    acc = _causal_sweep(i, logits, vt_ref, (sa_ref, sb_ref), m_ref, acc_ref)
    o_t = acc[:HEAD_DIM] / acc[HEAD_DIM:HEAD_DIM + 1]
    lp = lam_ref[...]
    lam = (jnp.exp(jnp.sum(lp[0:1] * lp[1:2], keepdims=True))
           - jnp.exp(jnp.sum(lp[2:3] * lp[3:4], keepdims=True)) + lam_init)
    od_t = o_t[:, :tq] - lam * o_t[:, tq:]
    y_t = od_t * lax.rsqrt(jnp.mean(od_t * od_t, axis=0, keepdims=True) + NORM_EPS) * sg_ref[...]
    o_ref[0] = (y_t * (1.0 - lam_init)).T.astype(o_ref.dtype)


def _diff_attn(qkv, bias_t, diff_lambda, diff_subln_col, layer):
    b, s, _ = qkv.shape
    n_bias = bias_t.shape[1] - 1
    assert s % KV_TILE == 0 and s % Q_TILE == 0
    lam_init = 0.8 - 0.6 * math.exp(-0.3 * layer)
    kern = functools.partial(_diff_attn_kernel, n_bias=n_bias, lam_init=lam_init, seq_len=s)
    return pl.pallas_call(
        kern,
        grid=(b, N_HEADS_DIFF, s // Q_TILE),
        in_specs=[
            pl.BlockSpec((1, Q_TILE, HEAD_DIM), lambda bi, h, i: (bi, i, _COL_QA + h)),
            pl.BlockSpec((1, s, HEAD_DIM), lambda bi, h, i: (bi, 0, _COL_KA + h)),
            pl.BlockSpec((1, s, HEAD_DIM), lambda bi, h, i: (bi, 0, _COL_VA + h)),
            pl.BlockSpec((1, n_bias + 1, ATTN_TILE, ATTN_TILE), lambda bi, h, i: (h, 0, 0, 0)),
            pl.BlockSpec((None, 4, DIFF_QK_DIM), lambda bi, h, i: (layer, 0, 0)),
            pl.BlockSpec((None, HEAD_DIM, 1), lambda bi, h, i: (layer, 0, 0)),
        ],
        out_specs=pl.BlockSpec((1, Q_TILE, HEAD_DIM), lambda bi, h, i: (bi, i, h)),
        out_shape=jax.ShapeDtypeStruct((b, s, W_DIFF), BF16),
        scratch_shapes=_sweep_scratch(s, 2 * Q_TILE),
        compiler_params=_params(("parallel", "parallel", "arbitrary")),
        name="diff_attn",
    )(qkv, qkv, qkv, bias_t, diff_lambda, diff_subln_col)


def _moba_kernel(q_ref, k_ref, v_ref, lanes_ref, bias_ref, o_ref, vt_ref, sa_ref, sb_ref, m_ref, acc_ref,
                 kmean_ref, *, n_bias, n_blocks, seq_len):
    tq, tk = Q_TILE, KV_TILE
    i = pl.program_id(2)

    @pl.when(i == 0)
    def _():
        _transpose_values(v_ref, vt_ref, seq_len)
        kf = k_ref[0].astype(F32).reshape(n_blocks, MOBA_BLOCK, HEAD_DIM)
        kmean_ref[...] = jnp.mean(kf, axis=1)

    q_t = q_ref[0].astype(F32).T
    gate = jnp.dot(kmean_ref[...], q_t, preferred_element_type=F32,
                   precision=lax.Precision.HIGHEST)
    blk = lax.broadcasted_iota(jnp.int32, gate.shape, 0)
    lane = lax.broadcasted_iota(jnp.int32, gate.shape, 1)
    own = i * (tq // MOBA_BLOCK)
    for e in range(1, tq // MOBA_BLOCK):
        own = own + (lane >= e * MOBA_BLOCK).astype(jnp.int32)
    past = blk < own
    gm = jnp.where(past, gate, NEG_INF)
    rank = jnp.zeros(gate.shape, jnp.int32)
    for jp in range(n_blocks):
        gj = gm[jp:jp + 1, :]
        beats = (gj > gm) | ((gj == gm) & (jp < blk))
        rank = rank + beats.astype(jnp.int32)
    allowed = (past & (rank < MOBA_TOPK)) | (blk == own)
    mask_t = jnp.where(allowed, 0.0, NEG_INF)
    mask_t = jnp.concatenate([mask_t, jnp.zeros((HEAD_DIM - n_blocks, tq), F32)], axis=0)
    q_aug_t = jnp.concatenate([q_t.astype(BF16), mask_t.astype(BF16)], axis=0)

    def logits(j, start):
        k_aug = jnp.concatenate([k_ref[0, pl.ds(start, tk), :], lanes_ref[pl.ds(start, tk), :]], axis=1)
        s_t = jnp.dot(k_aug, q_aug_t, preferred_element_type=F32)
        return s_t + _bias_block(bias_ref, i * (tq // ATTN_TILE), j * (tk // ATTN_TILE), n_bias)

    acc = _causal_sweep(i, logits, vt_ref, (sa_ref, sb_ref), m_ref, acc_ref)
    o_ref[0] = (acc[:HEAD_DIM] / acc[HEAD_DIM:HEAD_DIM + 1]).T.astype(o_ref.dtype)


def _moba_attn(qkv, bias_t):
    b, s, _ = qkv.shape
    n_bias = bias_t.shape[1] - 1
    n_blocks = s // MOBA_BLOCK
    assert s % KV_TILE == 0 and s % Q_TILE == 0 and n_blocks <= HEAD_DIM and n_blocks % 8 == 0
    block_lanes = np.zeros((s, HEAD_DIM), np.float32)
    block_lanes[np.arange(s), np.arange(s) // MOBA_BLOCK] = 1.0
    kern = functools.partial(_moba_kernel, n_bias=n_bias, n_blocks=n_blocks, seq_len=s)
    return pl.pallas_call(
        kern,
        grid=(b, N_HEADS_MOBA, s // Q_TILE),
        in_specs=[
            pl.BlockSpec((1, Q_TILE, HEAD_DIM), lambda bi, h, i: (bi, i, _COL_QB + h)),
            pl.BlockSpec((1, s, HEAD_DIM), lambda bi, h, i: (bi, 0, _COL_KB + h)),
            pl.BlockSpec((1, s, HEAD_DIM), lambda bi, h, i: (bi, 0, _COL_VB + h)),
            pl.BlockSpec((s, HEAD_DIM), lambda bi, h, i: (0, 0)),
            pl.BlockSpec((1, n_bias + 1, ATTN_TILE, ATTN_TILE), lambda bi, h, i: (N_HEADS_DIFF + h, 0, 0, 0)),
        ],
        out_specs=pl.BlockSpec((1, Q_TILE, HEAD_DIM), lambda bi, h, i: (bi, i, h)),
        out_shape=jax.ShapeDtypeStruct((b, s, W_MOBA), BF16),
        scratch_shapes=_sweep_scratch(s, Q_TILE) + [pltpu.VMEM((n_blocks, HEAD_DIM), F32)],
        compiler_params=_params(("parallel", "parallel", "arbitrary")),
        name="moba_attn",
    )(qkv, qkv, qkv, jnp.asarray(block_lanes, BF16), bias_t)


def _dil_batch(seq_len, dil):
    return min(DIL_BATCH, seq_len // dil // DIL_BLOCK)


def _dil_kernel(q_ref, k_ref, v_ref, bias_ref, o_ref, qf_ref, kf_ref, vf_ref, m_ref, l_ref, acc_ref,
                *, seq_len):
    blk = DIL_BLOCK
    chunk = KV_TILE

    kf_ref[0:DIL_PAD, :] = jnp.zeros((DIL_PAD, HEAD_DIM), F32)
    vf_ref[0:DIL_PAD, :] = jnp.zeros((DIL_PAD, HEAD_DIM), F32)

    def widen(c, carry):
        src = pl.ds(pl.multiple_of(c * chunk, chunk), chunk)
        dst = pl.ds(pl.multiple_of(DIL_PAD + c * chunk, chunk), chunk)
        qf_ref[src, :] = q_ref[0, src, :].astype(F32)
        kf_ref[dst, :] = k_ref[0, src, :].astype(F32)
        vf_ref[dst, :] = v_ref[0, src, :].astype(F32)
        return carry
    lax.fori_loop(0, seq_len // chunk, widen, 0)

    for br, (_, dil) in enumerate(DIL_PAIRS):
        nb = _dil_batch(seq_len, dil)
        rows = nb * blk
        trips_per_res = seq_len // dil // rows

        def rows_of(start, n, dil=dil):
            return pl.ds(start, n, stride=dil) if dil > 1 else pl.ds(start, n)

        def trip(idx, carry, br=br, dil=dil, nb=nb, rows=rows, trips_per_res=trips_per_res, rows_of=rows_of):
            res = idx // trips_per_res
            t = idx % trips_per_res
            tok = res + t * rows * dil
            q = qf_ref[rows_of(tok, rows), :].astype(BF16).reshape(nb, blk, HEAD_DIM)
            kk = kf_ref[rows_of(DIL_PAD + tok - blk * dil, rows + blk), :].astype(BF16)
            vv = vf_ref[rows_of(DIL_PAD + tok - blk * dil, rows + blk), :].astype(BF16)
            k_prev, k_cur = kk[:rows].reshape(nb, blk, HEAD_DIM), kk[blk:].reshape(nb, blk, HEAD_DIM)
            v_prev, v_cur = vv[:rows].reshape(nb, blk, HEAD_DIM), vv[blk:].reshape(nb, blk, HEAD_DIM)
            b_first = bias_ref[br, 0, jnp.where(t == 0, 2, 0)]
            b_prev = jnp.concatenate(
                [b_first[None], jnp.broadcast_to(bias_ref[br, 0, 0][None], (nb - 1, blk, blk))], axis=0)
            s_prev = jnp.einsum("nqd,nkd->nqk", q, k_prev, preferred_element_type=F32) + b_prev
            s_cur = jnp.einsum("nqd,nkd->nqk", q, k_cur, preferred_element_type=F32) + bias_ref[br, 0, 1][None]
            m_b = jnp.maximum(jnp.max(s_prev, axis=-1, keepdims=True), jnp.max(s_cur, axis=-1, keepdims=True))
            e_prev = jnp.exp(s_prev - m_b)
            e_cur = jnp.exp(s_cur - m_b)
            l_b = jnp.sum(e_prev, axis=-1, keepdims=True) + jnp.sum(e_cur, axis=-1, keepdims=True)
            o_b = (jnp.einsum("nqk,nkd->nqd", e_prev.astype(BF16), v_prev, preferred_element_type=F32)
                   + jnp.einsum("nqk,nkd->nqd", e_cur.astype(BF16), v_cur, preferred_element_type=F32))
            m_b = jnp.broadcast_to(m_b.reshape(rows, 1), (rows, HEAD_DIM))
            l_b = jnp.broadcast_to(l_b.reshape(rows, 1), (rows, HEAD_DIM))
            o_b = o_b.reshape(rows, HEAD_DIM)
            dst = rows_of(tok, rows)
            if br == 0:
                m_ref[dst, :], l_ref[dst, :], acc_ref[dst, :] = m_b, l_b, o_b
            else:
                m_old = m_ref[dst, :]
                m_new = jnp.maximum(m_old, m_b)
                w_old = jnp.exp(m_old - m_new)
                w_b = jnp.exp(m_b - m_new)
                m_ref[dst, :] = m_new
                l_ref[dst, :] = w_old * l_ref[dst, :] + w_b * l_b
                acc_ref[dst, :] = w_old * acc_ref[dst, :] + w_b * o_b
            return carry
        lax.fori_loop(0, dil * trips_per_res, trip, 0)

    def finish(c, carry):
        src = pl.ds(pl.multiple_of(c * chunk, chunk), chunk)
        o_ref[0, src, :] = (acc_ref[src, :] / l_ref[src, :]).astype(o_ref.dtype)
        return carry
    lax.fori_loop(0, seq_len // chunk, finish, 0)


def _dil_attn(qkv, bias):
    b, s, _ = qkv.shape
    n_br = len(DIL_PAIRS)
    assert all(s % (dil * _dil_batch(s, dil) * DIL_BLOCK) == 0 for _, dil in DIL_PAIRS) and s % KV_TILE == 0
    slab = lambda col: pl.BlockSpec((1, s, HEAD_DIM), lambda bi, h: (bi, 0, col + h))
    return pl.pallas_call(
        functools.partial(_dil_kernel, seq_len=s),
        grid=(b, N_HEADS_DIL),
        in_specs=[slab(_COL_QC), slab(_COL_QC + N_HEADS_DIL), slab(_COL_QC + 2 * N_HEADS_DIL),
                  pl.BlockSpec((n_br, 1, 3, DIL_BLOCK, DIL_BLOCK), lambda bi, h: (0, h, 0, 0, 0))],
        out_specs=pl.BlockSpec((1, s, HEAD_DIM), lambda bi, h: (bi, 0, h)),
        out_shape=jax.ShapeDtypeStruct((b, s, W_DIL), BF16),
        scratch_shapes=[pltpu.VMEM((s, HEAD_DIM), F32), pltpu.VMEM((DIL_PAD + s, HEAD_DIM), F32),
                        pltpu.VMEM((DIL_PAD + s, HEAD_DIM), F32), pltpu.VMEM((s, HEAD_DIM), F32),
                        pltpu.VMEM((s, HEAD_DIM), F32), pltpu.VMEM((s, HEAD_DIM), F32)],
        compiler_params=_params(("parallel", "parallel")),
        name="dil_attn",
    )(qkv, qkv, qkv, bias)


def _out_proj_kernel(x_ref, a_ref, b_ref, c_ref, w_ref, o_ref):
    acc = jnp.dot(a_ref[...], w_ref[0:W_DIFF, :], preferred_element_type=F32)
    acc += jnp.dot(b_ref[...], w_ref[W_DIFF:W_DIFF + W_MOBA, :], preferred_element_type=F32)
    acc += jnp.dot(c_ref[...], w_ref[W_DIFF + W_MOBA:, :], preferred_element_type=F32)
    o_ref[...] = x_ref[...] + acc


def _out_proj(x, o_a, o_b, o_c, w_out, layer, *, tm, tn):
    t, d = x.shape
    row = lambda width: pl.BlockSpec((tm, width), lambda i, j: (i, 0))
    return pl.pallas_call(
        _out_proj_kernel,
        grid=(t // tm, d // tn),
        in_specs=[pl.BlockSpec((tm, tn), lambda i, j: (i, j)), row(W_DIFF), row(W_MOBA), row(W_DIL),
                  pl.BlockSpec((None, MIX_WIDTH, tn), lambda i, j: (layer, 0, j))],
        out_specs=pl.BlockSpec((tm, tn), lambda i, j: (i, j)),
        out_shape=jax.ShapeDtypeStruct((t, d), F32),
        compiler_params=_params(("parallel", "parallel")),
        name="out_proj",
    )(x, o_a, o_b, o_c, w_out)


def _cross_attn_kernel(x_ref, g_ref, wq_ref, kv_ref, wo_ref, o_ref):
    x = x_ref[...]
    hn = _rms_norm(x, g_ref[...]).astype(BF16)
    q = jnp.dot(hn, wq_ref[...], preferred_element_type=F32).astype(BF16)
    kv = kv_ref[0]
    heads = []
    for h in range(N_MEM_HEADS):
        lo = h * HEAD_DIM
        s = lax.dot_general(q[:, lo:lo + HEAD_DIM], kv[:, lo:lo + HEAD_DIM], _NT,
                            preferred_element_type=F32)
        e = jnp.exp(s - jnp.max(s, axis=-1, keepdims=True))
        oh = jnp.dot(e.astype(BF16), kv[:, MEM_WIDTH + lo:MEM_WIDTH + lo + HEAD_DIM],
                     preferred_element_type=F32)
        heads.append((oh / jnp.sum(e, axis=-1, keepdims=True)).astype(BF16))
    o = jnp.concatenate(heads, axis=1)
    o_ref[...] = x + jnp.dot(o, wo_ref[...], preferred_element_type=F32)


def _cross_attn(x, g, w_cq, kv, w_co, layer, seq_len, *, tm):
    t, d = x.shape
    n_mem = kv.shape[1]
    tiles_per_batch = seq_len // tm
    return pl.pallas_call(
        _cross_attn_kernel,
        grid=(t // tm,),
        in_specs=[
            pl.BlockSpec((tm, d), lambda i: (i, 0)),
            pl.BlockSpec((None, 1, d), lambda i: (layer, 0, 0)),
            pl.BlockSpec((None, d, MEM_WIDTH), lambda i: (layer, 0, 0)),
            pl.BlockSpec((1, n_mem, 2 * MEM_WIDTH), lambda i: (i // tiles_per_batch, 0, 0)),
            pl.BlockSpec((None, MEM_WIDTH, d), lambda i: (layer, 0, 0)),
        ],
        out_specs=pl.BlockSpec((tm, d), lambda i: (i, 0)),
        out_shape=jax.ShapeDtypeStruct((t, d), F32),
        compiler_params=_params(("parallel",)),
        name="cross_attn",
    )(x, g, w_cq, kv, w_co)


def _ffn_kernel(x_ref, xp_ref, g_ref, wg_ref, wu_ref, cw_ref, cb_ref, wd_ref, gf_ref, o_ref, hn_ref,
                *, tm, tiles_per_batch, final_norm):
    i = pl.program_id(0)
    f = pl.program_id(1)

    @pl.when(f == 0)
    def _():
        g = g_ref[...]
        hn_ref[FFN_HALO:, :] = _rms_norm(x_ref[...], g).astype(BF16)
        prev = _rms_norm(xp_ref[...], g)
        prev = jnp.where(i % tiles_per_batch == 0, 0.0, prev)
        hn_ref[0:FFN_HALO, :] = prev.astype(BF16)
        o_ref[...] = x_ref[...]

    gate = jnp.dot(hn_ref[...], wg_ref[...], preferred_element_type=F32)
    up = jnp.dot(hn_ref[FFN_HALO:, :], wu_ref[...], preferred_element_type=F32)
    cw = cw_ref[...]
    conv = (cw[0:1] * gate[FFN_HALO - 2:FFN_HALO - 2 + tm]
            + cw[1:2] * gate[FFN_HALO - 1:FFN_HALO - 1 + tm]
            + cw[2:3] * gate[FFN_HALO:]) + cb_ref[...]
    act = conv * jax.nn.sigmoid(conv) * up
    o_ref[...] += jnp.dot(act.astype(BF16), wd_ref[...], preferred_element_type=F32)

    if final_norm:
        @pl.when(f == pl.num_programs(1) - 1)
        def _():
            o_ref[...] = _rms_norm(o_ref[...], gf_ref[...])


def _conv_ffn(x, g, w_up, conv_w, conv_b, w_down, g_final, layer, seq_len, *, tm, tf, final_norm):
    t, d = x.shape
    n_f = D_FF // tf
    tiles_per_batch = seq_len // tm
    halo_blocks = tm // FFN_HALO
    kern = functools.partial(_ffn_kernel, tm=tm, tiles_per_batch=tiles_per_batch, final_norm=final_norm)
    return pl.pallas_call(
        kern,
        grid=(t // tm, n_f),
        in_specs=[
            pl.BlockSpec((tm, d), lambda i, f: (i, 0)),
            pl.BlockSpec((FFN_HALO, d), lambda i, f: (jnp.maximum(i * halo_blocks - 1, 0), 0)),
            pl.BlockSpec((None, 1, d), lambda i, f: (layer, 0, 0)),
            pl.BlockSpec((None, d, tf), lambda i, f: (layer, 0, f)),
            pl.BlockSpec((None, d, tf), lambda i, f: (layer, 0, n_f + f)),
            pl.BlockSpec((None, CONV_WIDTH, tf), lambda i, f: (layer, 0, f)),
            pl.BlockSpec((None, 1, tf), lambda i, f: (layer, 0, f)),
            pl.BlockSpec((None, tf, d), lambda i, f: (layer, f, 0)),
            pl.BlockSpec((1, d), lambda i, f: (0, 0)),
        ],
        out_specs=pl.BlockSpec((tm, d), lambda i, f: (i, 0)),
        out_shape=jax.ShapeDtypeStruct((t, d), F32),
        scratch_shapes=[pltpu.VMEM((FFN_HALO + tm, d), BF16)],
        compiler_params=_params(("parallel", "arbitrary")),
        name="conv_ffn",
    )(x, x, g, w_up, w_up, conv_w, conv_b, w_down, g_final)


def _t5_bucket(dist):
    n = jnp.maximum(dist, 0)
    max_exact = N_BUCKETS // 2
    nf = jnp.maximum(n, 1).astype(F32)
    log_ratio = jnp.log(nf / max_exact) / math.log(REL_MAX_DIST / max_exact)
    large = max_exact + (log_ratio * (N_BUCKETS - max_exact)).astype(jnp.int32)
    large = jnp.minimum(large, N_BUCKETS - 1)
    return jnp.where(n < max_exact, n, large)


def _toeplitz(w, rows, cols, offset):
    length = cols + offset + 1
    assert w.shape[-1] == length and offset >= rows - 1
    lead = w.shape[:-1]
    flat = jnp.broadcast_to(w[..., None, :], lead + (rows, length)).reshape(lead + (rows * length,))
    shifted = flat[..., :rows * (length - 1)].reshape(lead + (rows, length - 1))
    return shifted[..., offset:offset + cols]


def _causal_bias_tiles(table, tile):
    n_tiles = -(-(REL_MAX_DIST + tile - 1) // tile) + 1
    dist = np.arange(-(tile - 1), n_tiles * tile + 1, dtype=np.int32)
    vec = jnp.where(dist[None, :] >= 0, table.astype(F32)[_t5_bucket(jnp.asarray(dist))].T, NEG_INF)
    tiles = _toeplitz(vec, tile, n_tiles * tile, tile - 1)
    tiles = tiles.reshape(-1, tile, n_tiles, tile).transpose(0, 2, 1, 3)
    return jnp.concatenate([tiles, jnp.full((tiles.shape[0], 1, tile, tile), NEG_INF, F32)], axis=1)


def _dilated_bias_tiles(table, window, dil):
    blk = DIL_BLOCK
    steps = window // dil
    j = np.arange(-(blk - 1), 2 * blk + 1, dtype=np.int32)
    valid = (j >= 0) & (j <= steps)
    bias = table.astype(F32)[_t5_bucket(jnp.asarray(np.maximum(j, 0) * dil))].T
    vec = jnp.where(valid[None, :], bias, NEG_INF)
    tiles = _toeplitz(vec, blk, 2 * blk, blk - 1)
    tiles = tiles.reshape(-1, blk, 2, blk).transpose(0, 2, 3, 1)
    masked = jnp.full((tiles.shape[0], 1, blk, blk), NEG_INF, F32)
    return jnp.concatenate([tiles[:, 1:2], tiles[:, 0:1], masked], axis=1)


def kernel(x, mem, w_in, w_out, g_mix, diff_lambda, diff_subln, rel_bias_table, g_cross, g_mem,
           w_cq, w_ckv, w_co, g_ffn, w_up, conv_w, conv_b, w_down, g_final):
    b, s, d = x.shape
    depth = w_in.shape[0]
    n_mem = mem.shape[1]
    t = b * s

    col_scale = np.ones((3 * MIX_WIDTH,), np.float32)
    col_scale[_COL_QA * HEAD_DIM:_COL_KA * HEAD_DIM] = DIFF_QK_DIM ** -0.5 * LOG2E
    col_scale[_COL_QB * HEAD_DIM:_COL_KB * HEAD_DIM] = HEAD_DIM ** -0.5 * LOG2E
    col_scale[_COL_QC * HEAD_DIM:(_COL_QC + N_HEADS_DIL) * HEAD_DIM] = HEAD_DIM ** -0.5
    w_in_b = (w_in * col_scale).astype(BF16)
    w_out_b = w_out.astype(BF16)
    w_cq_b = (w_cq * HEAD_DIM ** -0.5).astype(BF16)
    w_ckv_b = w_ckv.astype(BF16)
    w_co_b = w_co.astype(BF16)
    w_up_b = w_up.astype(BF16)
    w_down_b = w_down.astype(BF16)

    g_mix3 = g_mix.reshape(depth, 1, d)
    g_cross3 = g_cross.reshape(depth, 1, d)
    g_mem3 = g_mem.reshape(depth, 1, d)
    g_ffn3 = g_ffn.reshape(depth, 1, d)
    subln_col = diff_subln.reshape(depth, HEAD_DIM, 1)
    conv_b3 = conv_b.reshape(depth, 1, D_FF)
    g_final2 = g_final.reshape(1, d)

    n_ab = N_HEADS_DIFF + N_HEADS_MOBA
    bias_ab = _causal_bias_tiles(rel_bias_table[:, :n_ab] * LOG2E, ATTN_TILE)
    bias_c = jnp.stack([_dilated_bias_tiles(rel_bias_table[:, n_ab:], window, dil)
                        for window, dil in DIL_PAIRS])

    xf = x.reshape(t, d)
    memf = mem.reshape(b * n_mem, d)

    for layer in range(depth):
        qkv = _norm_matmul(xf, g_mix3, w_in_b, layer, tm=1024, tn=2048).reshape(b, s, 3 * MIX_WIDTH)
        o_a = _diff_attn(qkv, bias_ab, diff_lambda, subln_col, layer)
        o_b = _moba_attn(qkv, bias_ab)
        o_c = _dil_attn(qkv, bias_c)
        xf = _out_proj(xf, o_a.reshape(t, W_DIFF), o_b.reshape(t, W_MOBA), o_c.reshape(t, W_DIL),
                       w_out_b, layer, tm=1024, tn=1024)
        kv = _norm_matmul(memf, g_mem3, w_ckv_b, layer, tm=b * n_mem, tn=512)
        xf = _cross_attn(xf, g_cross3, w_cq_b, kv.reshape(b, n_mem, 2 * MEM_WIDTH), w_co_b, layer, s,
                         tm=512)
        xf = _conv_ffn(xf, g_ffn3, w_up_b, conv_w, conv_b3, w_down_b, g_final2, layer, s,
                       tm=512, tf=512, final_norm=(layer == depth - 1))
    return xf.reshape(b, s, d)
```

```python
import functools
import math

import numpy as np
import jax
import jax.numpy as jnp
from jax import lax
from jax.experimental import pallas as pl
from jax.experimental.pallas import tpu as pltpu

F32 = jnp.float32
BF16 = jnp.bfloat16

HEAD_DIM = 128
N_HEADS_DIFF = 6
N_HEADS_MOBA = 4
N_HEADS_DIL = 6
W_DIFF = N_HEADS_DIFF * HEAD_DIM
W_MOBA = N_HEADS_MOBA * HEAD_DIM
W_DIL = N_HEADS_DIL * HEAD_DIM
MIX_WIDTH = W_DIFF + W_MOBA + W_DIL
DIFF_QK_DIM = HEAD_DIM // 2
MOBA_BLOCK = 256
MOBA_TOPK = 3
DIL_PAIRS = ((128, 1), (512, 4), (2048, 16))
DIL_BLOCK = 128
N_BUCKETS = 32
REL_MAX_DIST = 2048
N_MEM_HEADS = 4
MEM_WIDTH = N_MEM_HEADS * HEAD_DIM
D_FF = 5632
CONV_WIDTH = 3
NORM_EPS = 1e-6
NEG_INF = -1e30

_COL_QA = 0
_COL_KA = _COL_QA + N_HEADS_DIFF
_COL_VA = _COL_KA + N_HEADS_DIFF
_COL_QB = _COL_VA + N_HEADS_DIFF
_COL_KB = _COL_QB + N_HEADS_MOBA
_COL_VB = _COL_KB + N_HEADS_MOBA
_COL_QC = _COL_VB + N_HEADS_MOBA

ATTN_TILE = 256
Q_TILE = 1024
KV_TILE = 1024
VT_ROWS = HEAD_DIM + 16
LOG2E = math.log2(math.e)
DIL_BATCH = 8
DIL_PAD = DIL_BLOCK * max(dil for _, dil in DIL_PAIRS)
FFN_HALO = 16
VMEM_LIMIT = 56 * 1024 * 1024

_NT = (((1,), (1,)), ((), ()))


def _params(semantics):
    return pltpu.CompilerParams(dimension_semantics=semantics, vmem_limit_bytes=VMEM_LIMIT)


def _rms_norm(xf, g):
    y = xf * lax.rsqrt(jnp.mean(xf * xf, axis=-1, keepdims=True) + NORM_EPS)
    return y * g


def _norm_matmul_kernel(x_ref, g_ref, w_ref, o_ref, hn_ref):
    @pl.when(pl.program_id(1) == 0)
    def _():
        hn_ref[...] = _rms_norm(x_ref[...], g_ref[...]).astype(BF16)

    o_ref[...] = jnp.dot(hn_ref[...], w_ref[...], preferred_element_type=F32).astype(o_ref.dtype)


def _norm_matmul(x, g, w, layer, *, tm, tn):
    t, d = x.shape
    n = w.shape[-1]
    return pl.pallas_call(
        _norm_matmul_kernel,
        grid=(t // tm, n // tn),
        in_specs=[
            pl.BlockSpec((tm, d), lambda i, j: (i, 0)),
            pl.BlockSpec((None, 1, d), lambda i, j: (layer, 0, 0)),
            pl.BlockSpec((None, d, tn), lambda i, j: (layer, 0, j)),
        ],
        out_specs=pl.BlockSpec((tm, tn), lambda i, j: (i, j)),
        out_shape=jax.ShapeDtypeStruct((t, n), BF16),
        scratch_shapes=[pltpu.VMEM((tm, d), BF16)],
        compiler_params=_params(("parallel", "arbitrary")),
        name="norm_matmul",
    )(x, g, w)


def _bias_block(bias_ref, q_blk, k_blk, n_bias):
    cols = []
    for e in range(Q_TILE // ATTN_TILE):
        tiles = []
        for c in range(KV_TILE // ATTN_TILE):
            d = (q_blk + e) - (k_blk + c)
            tiles.append(bias_ref[0, jnp.where(d < 0, n_bias, jnp.minimum(d, n_bias - 1))])
        cols.append(jnp.concatenate(tiles, axis=0))
    return jnp.concatenate(cols, axis=1)


def _causal_sweep(i, logits, vt_ref, s_refs, m_ref, acc_ref):
    tk = KV_TILE
    n_steps = ((i + 1) * Q_TILE - 1) // tk + 1

    def key_start(j):
        return pl.multiple_of(j * tk, tk)

    def logits_into(j, dst_ref):
        s_t = logits(j, key_start(j))
        dst_ref[0:tk, :] = s_t
        dst_ref[tk:tk + 1, :] = jnp.max(s_t, axis=0, keepdims=True)

    def step(j, slot, prefetch):
        if prefetch:
            logits_into(j + 1, s_refs[1 - slot])
        m = m_ref[...]
        m_new = jnp.maximum(m, s_refs[slot][tk:tk + 1, :])
        p = jnp.exp2((s_refs[slot][0:tk, :] - m_new).astype(BF16))
        pv = jnp.dot(vt_ref[:, pl.ds(key_start(j), tk)], p, preferred_element_type=F32)
        acc_ref[...] = jnp.exp2(m - m_new) * acc_ref[...] + pv
        m_ref[...] = m_new

    m_ref[...] = jnp.full(m_ref.shape, NEG_INF, F32)
    acc_ref[...] = jnp.zeros(acc_ref.shape, F32)
    logits_into(0, s_refs[0])

    def body(j, carry):
        for slot in range(2):
            @pl.when(j % 2 == slot)
            def _(slot=slot):
                step(j, slot, True)
        return carry

    def diagonal_step(j, slot):
        m = m_ref[...]
        m_new = jnp.maximum(m, s_refs[slot][tk:tk + 1, :])
        acc_ref[...] = jnp.exp2(m - m_new) * acc_ref[...]
        for c in range(tk // ATTN_TILE):
            rows = slice(c * ATTN_TILE, (c + 1) * ATTN_TILE)
            vt_c = vt_ref[:, pl.ds(pl.multiple_of(j * tk + c * ATTN_TILE, ATTN_TILE), ATTN_TILE)]
            for lo in range(0, m_ref.shape[-1], Q_TILE):
                lanes = slice(lo + c * ATTN_TILE, lo + Q_TILE)
                p = jnp.exp2((s_refs[slot][rows, lanes] - m_new[:, lanes]).astype(BF16))
                acc_ref[:, lanes] += jnp.dot(vt_c, p, preferred_element_type=F32)
        m_ref[...] = m_new

    last_step = diagonal_step if Q_TILE == KV_TILE else functools.partial(step, prefetch=False)
    lax.fori_loop(0, n_steps - 1, body, 0)
    for slot in range(2):
        @pl.when((n_steps - 1) % 2 == slot)
        def _(slot=slot):
            last_step(n_steps - 1, slot)
    return acc_ref[...]


def _sweep_scratch(seq_len, n):
    return [pltpu.VMEM((VT_ROWS, seq_len), BF16), pltpu.VMEM((KV_TILE + 8, n), F32),
            pltpu.VMEM((KV_TILE + 8, n), F32), pltpu.VMEM((1, n), F32), pltpu.VMEM((VT_ROWS, n), F32)]


def _transpose_values(v_ref, vt_ref, seq_len):
    def body(c, carry):
        start = pl.multiple_of(c * KV_TILE, KV_TILE)
        vt_ref[0:HEAD_DIM, pl.ds(start, KV_TILE)] = (
            v_ref[0, pl.ds(start, KV_TILE), :].astype(F32).T.astype(BF16))
        return carry
    lax.fori_loop(0, seq_len // KV_TILE, body, 0)
    vt_ref[HEAD_DIM:, :] = jnp.ones((VT_ROWS - HEAD_DIM, seq_len), BF16)


def _diff_attn_kernel(q_ref, k_ref, v_ref, bias_ref, lam_ref, sg_ref, o_ref, vt_ref, sa_ref, sb_ref,
                      m_ref, acc_ref, *, n_bias, lam_init, seq_len):
    tq, tk = Q_TILE, KV_TILE
    i = pl.program_id(2)

    @pl.when(i == 0)
    def _():
        _transpose_values(v_ref, vt_ref, seq_len)

    q_t = q_ref[0].astype(F32).T
    row = lax.broadcasted_iota(jnp.int32, q_t.shape, 0)
    qm_t = jnp.concatenate([jnp.where(row < DIFF_QK_DIM, q_t, 0.0),
                            jnp.where(row >= DIFF_QK_DIM, q_t, 0.0)], axis=1).astype(BF16)

    def logits(j, start):
        s_t = jnp.dot(k_ref[0, pl.ds(start, tk), :], qm_t, preferred_element_type=F32)
        b_t = _bias_block(bias_ref, i * (tq // ATTN_TILE), j * (tk // ATTN_TILE), n_bias)
        return s_t + jnp.concatenate([b_t, b_t], axis=1)

    acc = _causal_sweep(i, logits, vt_ref, (sa_ref, sb_ref), m_ref, acc_ref)
    o_t = acc[:HEAD_DIM] / acc[HEAD_DIM:HEAD_DIM + 1]
    lp = lam_ref[...]
    lam = (jnp.exp(jnp.sum(lp[0:1] * lp[1:2], keepdims=True))
           - jnp.exp(jnp.sum(lp[2:3] * lp[3:4], keepdims=True)) + lam_init)
    od_t = o_t[:, :tq] - lam * o_t[:, tq:]
    y_t = od_t * lax.rsqrt(jnp.mean(od_t * od_t, axis=0, keepdims=True) + NORM_EPS) * sg_ref[...]
    o_ref[0] = (y_t * (1.0 - lam_init)).T.astype(o_ref.dtype)


def _diff_attn(qkv, bias_t, diff_lambda, diff_subln_col, layer):
    b, s, _ = qkv.shape
    n_bias = bias_t.shape[1] - 1
    assert s % KV_TILE == 0 and s % Q_TILE == 0
    lam_init = 0.8 - 0.6 * math.exp(-0.3 * layer)
    kern = functools.partial(_diff_attn_kernel, n_bias=n_bias, lam_init=lam_init, seq_len=s)
    return pl.pallas_call(
        kern,
        grid=(b, N_HEADS_DIFF, s // Q_TILE),
        in_specs=[
            pl.BlockSpec((1, Q_TILE, HEAD_DIM), lambda bi, h, i: (bi, i, _COL_QA + h)),
            pl.BlockSpec((1, s, HEAD_DIM), lambda bi, h, i: (bi, 0, _COL_KA + h)),
            pl.BlockSpec((1, s, HEAD_DIM), lambda bi, h, i: (bi, 0, _COL_VA + h)),
            pl.BlockSpec((1, n_bias + 1, ATTN_TILE, ATTN_TILE), lambda bi, h, i: (h, 0, 0, 0)),
            pl.BlockSpec((None, 4, DIFF_QK_DIM), lambda bi, h, i: (layer, 0, 0)),
            pl.BlockSpec((None, HEAD_DIM, 1), lambda bi, h, i: (layer, 0, 0)),
        ],
        out_specs=pl.BlockSpec((1, Q_TILE, HEAD_DIM), lambda bi, h, i: (bi, i, h)),
        out_shape=jax.ShapeDtypeStruct((b, s, W_DIFF), BF16),
        scratch_shapes=_sweep_scratch(s, 2 * Q_TILE),
        compiler_params=_params(("parallel", "parallel", "arbitrary")),
        name="diff_attn",
    )(qkv, qkv, qkv, bias_t, diff_lambda, diff_subln_col)


def _moba_kernel(q_ref, k_ref, v_ref, lanes_ref, bias_ref, o_ref, vt_ref, sa_ref, sb_ref, m_ref, acc_ref,
                 kmean_ref, *, n_bias, n_blocks, seq_len):
    tq, tk = Q_TILE, KV_TILE
    i = pl.program_id(2)

    @pl.when(i == 0)
    def _():
        _transpose_values(v_ref, vt_ref, seq_len)
        kf = k_ref[0].astype(F32).reshape(n_blocks, MOBA_BLOCK, HEAD_DIM)
        kmean_ref[...] = jnp.mean(kf, axis=1)

    q_t = q_ref[0].astype(F32).T
    gate = jnp.dot(kmean_ref[...], q_t, preferred_element_type=F32,
                   precision=lax.Precision.HIGHEST)
    blk = lax.broadcasted_iota(jnp.int32, gate.shape, 0)
    lane = lax.broadcasted_iota(jnp.int32, gate.shape, 1)
    own = i * (tq // MOBA_BLOCK)
    for e in range(1, tq // MOBA_BLOCK):
        own = own + (lane >= e * MOBA_BLOCK).astype(jnp.int32)
    past = blk < own
    gm = jnp.where(past, gate, NEG_INF)
    rank = jnp.zeros(gate.shape, jnp.int32)
    for jp in range(n_blocks):
        gj = gm[jp:jp + 1, :]
        beats = (gj > gm) | ((gj == gm) & (jp < blk))
        rank = rank + beats.astype(jnp.int32)
    allowed = (past & (rank < MOBA_TOPK)) | (blk == own)
    mask_t = jnp.where(allowed, 0.0, NEG_INF)
    mask_t = jnp.concatenate([mask_t, jnp.zeros((HEAD_DIM - n_blocks, tq), F32)], axis=0)
    q_aug_t = jnp.concatenate([q_t.astype(BF16), mask_t.astype(BF16)], axis=0)

    def logits(j, start):
        k_aug = jnp.concatenate([k_ref[0, pl.ds(start, tk), :], lanes_ref[pl.ds(start, tk), :]], axis=1)
        s_t = jnp.dot(k_aug, q_aug_t, preferred_element_type=F32)
        return s_t + _bias_block(bias_ref, i * (tq // ATTN_TILE), j * (tk // ATTN_TILE), n_bias)

    acc = _causal_sweep(i, logits, vt_ref, (sa_ref, sb_ref), m_ref, acc_ref)
    o_ref[0] = (acc[:HEAD_DIM] / acc[HEAD_DIM:HEAD_DIM + 1]).T.astype(o_ref.dtype)


def _moba_attn(qkv, bias_t):
    b, s, _ = qkv.shape
    n_bias = bias_t.shape[1] - 1
    n_blocks = s // MOBA_BLOCK
    assert s % KV_TILE == 0 and s % Q_TILE == 0 and n_blocks <= HEAD_DIM and n_blocks % 8 == 0
    block_lanes = np.zeros((s, HEAD_DIM), np.float32)
    block_lanes[np.arange(s), np.arange(s) // MOBA_BLOCK] = 1.0
    kern = functools.partial(_moba_kernel, n_bias=n_bias, n_blocks=n_blocks, seq_len=s)
    return pl.pallas_call(
        kern,
        grid=(b, N_HEADS_MOBA, s // Q_TILE),
        in_specs=[
            pl.BlockSpec((1, Q_TILE, HEAD_DIM), lambda bi, h, i: (bi, i, _COL_QB + h)),
            pl.BlockSpec((1, s, HEAD_DIM), lambda bi, h, i: (bi, 0, _COL_KB + h)),
            pl.BlockSpec((1, s, HEAD_DIM), lambda bi, h, i: (bi, 0, _COL_VB + h)),
            pl.BlockSpec((s, HEAD_DIM), lambda bi, h, i: (0, 0)),
            pl.BlockSpec((1, n_bias + 1, ATTN_TILE, ATTN_TILE), lambda bi, h, i: (N_HEADS_DIFF + h, 0, 0, 0)),
        ],
        out_specs=pl.BlockSpec((1, Q_TILE, HEAD_DIM), lambda bi, h, i: (bi, i, h)),
        out_shape=jax.ShapeDtypeStruct((b, s, W_MOBA), BF16),
        scratch_shapes=_sweep_scratch(s, Q_TILE) + [pltpu.VMEM((n_blocks, HEAD_DIM), F32)],
        compiler_params=_params(("parallel", "parallel", "arbitrary")),
        name="moba_attn",
    )(qkv, qkv, qkv, jnp.asarray(block_lanes, BF16), bias_t)


def _dil_batch(seq_len, dil):
    return min(DIL_BATCH, seq_len // dil // DIL_BLOCK)


def _dil_kernel(q_ref, k_ref, v_ref, bias_ref, o_ref, qf_ref, kf_ref, vf_ref, m_ref, l_ref, acc_ref,
                *, seq_len):
    blk = DIL_BLOCK
    chunk = KV_TILE

    kf_ref[0:DIL_PAD, :] = jnp.zeros((DIL_PAD, HEAD_DIM), F32)
    vf_ref[0:DIL_PAD, :] = jnp.zeros((DIL_PAD, HEAD_DIM), F32)

    def widen(c, carry):
        src = pl.ds(pl.multiple_of(c * chunk, chunk), chunk)
        dst = pl.ds(pl.multiple_of(DIL_PAD + c * chunk, chunk), chunk)
        qf_ref[src, :] = q_ref[0, src, :].astype(F32)
        kf_ref[dst, :] = k_ref[0, src, :].astype(F32)
        vf_ref[dst, :] = v_ref[0, src, :].astype(F32)
        return carry
    lax.fori_loop(0, seq_len // chunk, widen, 0)

    for br, (_, dil) in enumerate(DIL_PAIRS):
        nb = _dil_batch(seq_len, dil)
        rows = nb * blk
        trips_per_res = seq_len // dil // rows

        def rows_of(start, n, dil=dil):
            return pl.ds(start, n, stride=dil) if dil > 1 else pl.ds(start, n)

        def trip(idx, carry, br=br, dil=dil, nb=nb, rows=rows, trips_per_res=trips_per_res, rows_of=rows_of):
            res = idx // trips_per_res
            t = idx % trips_per_res
            tok = res + t * rows * dil
            q = qf_ref[rows_of(tok, rows), :].astype(BF16).reshape(nb, blk, HEAD_DIM)
            kk = kf_ref[rows_of(DIL_PAD + tok - blk * dil, rows + blk), :].astype(BF16)
            vv = vf_ref[rows_of(DIL_PAD + tok - blk * dil, rows + blk), :].astype(BF16)
            k_prev, k_cur = kk[:rows].reshape(nb, blk, HEAD_DIM), kk[blk:].reshape(nb, blk, HEAD_DIM)
            v_prev, v_cur = vv[:rows].reshape(nb, blk, HEAD_DIM), vv[blk:].reshape(nb, blk, HEAD_DIM)
            b_first = bias_ref[br, 0, jnp.where(t == 0, 2, 0)]
            b_prev = jnp.concatenate(
                [b_first[None], jnp.broadcast_to(bias_ref[br, 0, 0][None], (nb - 1, blk, blk))], axis=0)
            s_prev = jnp.einsum("nqd,nkd->nqk", q, k_prev, preferred_element_type=F32) + b_prev
            s_cur = jnp.einsum("nqd,nkd->nqk", q, k_cur, preferred_element_type=F32) + bias_ref[br, 0, 1][None]
            m_b = jnp.maximum(jnp.max(s_prev, axis=-1, keepdims=True), jnp.max(s_cur, axis=-1, keepdims=True))
            e_prev = jnp.exp(s_prev - m_b)
            e_cur = jnp.exp(s_cur - m_b)
            l_b = jnp.sum(e_prev, axis=-1, keepdims=True) + jnp.sum(e_cur, axis=-1, keepdims=True)
            o_b = (jnp.einsum("nqk,nkd->nqd", e_prev.astype(BF16), v_prev, preferred_element_type=F32)
                   + jnp.einsum("nqk,nkd->nqd", e_cur.astype(BF16), v_cur, preferred_element_type=F32))
            m_b = jnp.broadcast_to(m_b.reshape(rows, 1), (rows, HEAD_DIM))
            l_b = jnp.broadcast_to(l_b.reshape(rows, 1), (rows, HEAD_DIM))
            o_b = o_b.reshape(rows, HEAD_DIM)
            dst = rows_of(tok, rows)
            if br == 0:
                m_ref[dst, :], l_ref[dst, :], acc_ref[dst, :] = m_b, l_b, o_b
            else:
                m_old = m_ref[dst, :]
                m_new = jnp.maximum(m_old, m_b)
                w_old = jnp.exp(m_old - m_new)
                w_b = jnp.exp(m_b - m_new)
                m_ref[dst, :] = m_new
                l_ref[dst, :] = w_old * l_ref[dst, :] + w_b * l_b
                acc_ref[dst, :] = w_old * acc_ref[dst, :] + w_b * o_b
            return carry
        lax.fori_loop(0, dil * trips_per_res, trip, 0)

    def finish(c, carry):
        src = pl.ds(pl.multiple_of(c * chunk, chunk), chunk)
        o_ref[0, src, :] = (acc_ref[src, :] / l_ref[src, :]).astype(o_ref.dtype)
        return carry
    lax.fori_loop(0, seq_len // chunk, finish, 0)


def _dil_attn(qkv, bias):
    b, s, _ = qkv.shape
    n_br = len(DIL_PAIRS)
    assert all(s % (dil * _dil_batch(s, dil) * DIL_BLOCK) == 0 for _, dil in DIL_PAIRS) and s % KV_TILE == 0
    slab = lambda col: pl.BlockSpec((1, s, HEAD_DIM), lambda bi, h: (bi, 0, col + h))
    return pl.pallas_call(
        functools.partial(_dil_kernel, seq_len=s),
        grid=(b, N_HEADS_DIL),
        in_specs=[slab(_COL_QC), slab(_COL_QC + N_HEADS_DIL), slab(_COL_QC + 2 * N_HEADS_DIL),
                  pl.BlockSpec((n_br, 1, 3, DIL_BLOCK, DIL_BLOCK), lambda bi, h: (0, h, 0, 0, 0))],
        out_specs=pl.BlockSpec((1, s, HEAD_DIM), lambda bi, h: (bi, 0, h)),
        out_shape=jax.ShapeDtypeStruct((b, s, W_DIL), BF16),
        scratch_shapes=[pltpu.VMEM((s, HEAD_DIM), F32), pltpu.VMEM((DIL_PAD + s, HEAD_DIM), F32),
                        pltpu.VMEM((DIL_PAD + s, HEAD_DIM), F32), pltpu.VMEM((s, HEAD_DIM), F32),
                        pltpu.VMEM((s, HEAD_DIM), F32), pltpu.VMEM((s, HEAD_DIM), F32)],
        compiler_params=_params(("parallel", "parallel")),
        name="dil_attn",
    )(qkv, qkv, qkv, bias)


def _out_proj_kernel(x_ref, a_ref, b_ref, c_ref, w_ref, o_ref):
    acc = jnp.dot(a_ref[...], w_ref[0:W_DIFF, :], preferred_element_type=F32)
    acc += jnp.dot(b_ref[...], w_ref[W_DIFF:W_DIFF + W_MOBA, :], preferred_element_type=F32)
    acc += jnp.dot(c_ref[...], w_ref[W_DIFF + W_MOBA:, :], preferred_element_type=F32)
    o_ref[...] = x_ref[...] + acc


def _out_proj(x, o_a, o_b, o_c, w_out, layer, *, tm, tn):
    t, d = x.shape
    row = lambda width: pl.BlockSpec((tm, width), lambda i, j: (i, 0))
    return pl.pallas_call(
        _out_proj_kernel,
        grid=(t // tm, d // tn),
        in_specs=[pl.BlockSpec((tm, tn), lambda i, j: (i, j)), row(W_DIFF), row(W_MOBA), row(W_DIL),
                  pl.BlockSpec((None, MIX_WIDTH, tn), lambda i, j: (layer, 0, j))],
        out_specs=pl.BlockSpec((tm, tn), lambda i, j: (i, j)),
        out_shape=jax.ShapeDtypeStruct((t, d), F32),
        compiler_params=_params(("parallel", "parallel")),
        name="out_proj",
    )(x, o_a, o_b, o_c, w_out)


def _cross_attn_kernel(x_ref, g_ref, wq_ref, kv_ref, wo_ref, o_ref):
    x = x_ref[...]
    hn = _rms_norm(x, g_ref[...]).astype(BF16)
    q = jnp.dot(hn, wq_ref[...], preferred_element_type=F32).astype(BF16)
    kv = kv_ref[0]
    heads = []
    for h in range(N_MEM_HEADS):
        lo = h * HEAD_DIM
        s = lax.dot_general(q[:, lo:lo + HEAD_DIM], kv[:, lo:lo + HEAD_DIM], _NT,
                            preferred_element_type=F32)
        e = jnp.exp(s - jnp.max(s, axis=-1, keepdims=True))
        oh = jnp.dot(e.astype(BF16), kv[:, MEM_WIDTH + lo:MEM_WIDTH + lo + HEAD_DIM],
                     preferred_element_type=F32)
        heads.append((oh / jnp.sum(e, axis=-1, keepdims=True)).astype(BF16))
    o = jnp.concatenate(heads, axis=1)
    o_ref[...] = x + jnp.dot(o, wo_ref[...], preferred_element_type=F32)


def _cross_attn(x, g, w_cq, kv, w_co, layer, seq_len, *, tm):
    t, d = x.shape
    n_mem = kv.shape[1]
    tiles_per_batch = seq_len // tm
    return pl.pallas_call(
        _cross_attn_kernel,
        grid=(t // tm,),
        in_specs=[
            pl.BlockSpec((tm, d), lambda i: (i, 0)),
            pl.BlockSpec((None, 1, d), lambda i: (layer, 0, 0)),
            pl.BlockSpec((None, d, MEM_WIDTH), lambda i: (layer, 0, 0)),
            pl.BlockSpec((1, n_mem, 2 * MEM_WIDTH), lambda i: (i // tiles_per_batch, 0, 0)),
            pl.BlockSpec((None, MEM_WIDTH, d), lambda i: (layer, 0, 0)),
        ],
        out_specs=pl.BlockSpec((tm, d), lambda i: (i, 0)),
        out_shape=jax.ShapeDtypeStruct((t, d), F32),
        compiler_params=_params(("parallel",)),
        name="cross_attn",
    )(x, g, w_cq, kv, w_co)


def _ffn_kernel(x_ref, xp_ref, g_ref, wg_ref, wu_ref, cw_ref, cb_ref, wd_ref, gf_ref, o_ref, hn_ref,
                *, tm, tiles_per_batch, final_norm):
    i = pl.program_id(0)
    f = pl.program_id(1)

    @pl.when(f == 0)
    def _():
        g = g_ref[...]
        hn_ref[FFN_HALO:, :] = _rms_norm(x_ref[...], g).astype(BF16)
        prev = _rms_norm(xp_ref[...], g)
        prev = jnp.where(i % tiles_per_batch == 0, 0.0, prev)
        hn_ref[0:FFN_HALO, :] = prev.astype(BF16)
        o_ref[...] = x_ref[...]

    gate = jnp.dot(hn_ref[...], wg_ref[...], preferred_element_type=F32)
    up = jnp.dot(hn_ref[FFN_HALO:, :], wu_ref[...], preferred_element_type=F32)
    cw = cw_ref[...]
    conv = (cw[0:1] * gate[FFN_HALO - 2:FFN_HALO - 2 + tm]
            + cw[1:2] * gate[FFN_HALO - 1:FFN_HALO - 1 + tm]
            + cw[2:3] * gate[FFN_HALO:]) + cb_ref[...]
    act = conv * jax.nn.sigmoid(conv) * up
    o_ref[...] += jnp.dot(act.astype(BF16), wd_ref[...], preferred_element_type=F32)

    if final_norm:
        @pl.when(f == pl.num_programs(1) - 1)
        def _():
            o_ref[...] = _rms_norm(o_ref[...], gf_ref[...])


def _conv_ffn(x, g, w_up, conv_w, conv_b, w_down, g_final, layer, seq_len, *, tm, tf, final_norm):
    t, d = x.shape
    n_f = D_FF // tf
    tiles_per_batch = seq_len // tm
    halo_blocks = tm // FFN_HALO
    kern = functools.partial(_ffn_kernel, tm=tm, tiles_per_batch=tiles_per_batch, final_norm=final_norm)
    return pl.pallas_call(
        kern,
        grid=(t // tm, n_f),
        in_specs=[
            pl.BlockSpec((tm, d), lambda i, f: (i, 0)),
            pl.BlockSpec((FFN_HALO, d), lambda i, f: (jnp.maximum(i * halo_blocks - 1, 0), 0)),
            pl.BlockSpec((None, 1, d), lambda i, f: (layer, 0, 0)),
            pl.BlockSpec((None, d, tf), lambda i, f: (layer, 0, f)),
            pl.BlockSpec((None, d, tf), lambda i, f: (layer, 0, n_f + f)),
            pl.BlockSpec((None, CONV_WIDTH, tf), lambda i, f: (layer, 0, f)),
            pl.BlockSpec((None, 1, tf), lambda i, f: (layer, 0, f)),
            pl.BlockSpec((None, tf, d), lambda i, f: (layer, f, 0)),
            pl.BlockSpec((1, d), lambda i, f: (0, 0)),
        ],
        out_specs=pl.BlockSpec((tm, d), lambda i, f: (i, 0)),
        out_shape=jax.ShapeDtypeStruct((t, d), F32),
        scratch_shapes=[pltpu.VMEM((FFN_HALO + tm, d), BF16)],
        compiler_params=_params(("parallel", "arbitrary")),
        name="conv_ffn",
    )(x, x, g, w_up, w_up, conv_w, conv_b, w_down, g_final)


def _t5_bucket(dist):
    n = jnp.maximum(dist, 0)
    max_exact = N_BUCKETS // 2
    nf = jnp.maximum(n, 1).astype(F32)
    log_ratio = jnp.log(nf / max_exact) / math.log(REL_MAX_DIST / max_exact)
    large = max_exact + (log_ratio * (N_BUCKETS - max_exact)).astype(jnp.int32)
    large = jnp.minimum(large, N_BUCKETS - 1)
    return jnp.where(n < max_exact, n, large)


def _toeplitz(w, rows, cols, offset):
    length = cols + offset + 1
    assert w.shape[-1] == length and offset >= rows - 1
    lead = w.shape[:-1]
    flat = jnp.broadcast_to(w[..., None, :], lead + (rows, length)).reshape(lead + (rows * length,))
    shifted = flat[..., :rows * (length - 1)].reshape(lead + (rows, length - 1))
    return shifted[..., offset:offset + cols]


def _causal_bias_tiles(table, tile):
    n_tiles = -(-(REL_MAX_DIST + tile - 1) // tile) + 1
    dist = np.arange(-(tile - 1), n_tiles * tile + 1, dtype=np.int32)
    vec = jnp.where(dist[None, :] >= 0, table.astype(F32)[_t5_bucket(jnp.asarray(dist))].T, NEG_INF)
    tiles = _toeplitz(vec, tile, n_tiles * tile, tile - 1)
    tiles = tiles.reshape(-1, tile, n_tiles, tile).transpose(0, 2, 1, 3)
    return jnp.concatenate([tiles, jnp.full((tiles.shape[0], 1, tile, tile), NEG_INF, F32)], axis=1)


def _dilated_bias_tiles(table, window, dil):
    blk = DIL_BLOCK
    steps = window // dil
    j = np.arange(-(blk - 1), 2 * blk + 1, dtype=np.int32)
    valid = (j >= 0) & (j <= steps)
    bias = table.astype(F32)[_t5_bucket(jnp.asarray(np.maximum(j, 0) * dil))].T
    vec = jnp.where(valid[None, :], bias, NEG_INF)
    tiles = _toeplitz(vec, blk, 2 * blk, blk - 1)
    tiles = tiles.reshape(-1, blk, 2, blk).transpose(0, 2, 3, 1)
    masked = jnp.full((tiles.shape[0], 1, blk, blk), NEG_INF, F32)
    return jnp.concatenate([tiles[:, 1:2], tiles[:, 0:1], masked], axis=1)


def kernel(x, mem, w_in, w_out, g_mix, diff_lambda, diff_subln, rel_bias_table, g_cross, g_mem,
           w_cq, w_ckv, w_co, g_ffn, w_up, conv_w, conv_b, w_down, g_final):
    b, s, d = x.shape
    depth = w_in.shape[0]
    n_mem = mem.shape[1]
    t = b * s

    col_scale = np.ones((3 * MIX_WIDTH,), np.float32)
    col_scale[_COL_QA * HEAD_DIM:_COL_KA * HEAD_DIM] = DIFF_QK_DIM ** -0.5 * LOG2E
    col_scale[_COL_QB * HEAD_DIM:_COL_KB * HEAD_DIM] = HEAD_DIM ** -0.5 * LOG2E
    col_scale[_COL_QC * HEAD_DIM:(_COL_QC + N_HEADS_DIL) * HEAD_DIM] = HEAD_DIM ** -0.5
    w_in_b = (w_in * col_scale).astype(BF16)
    w_out_b = w_out.astype(BF16)
    w_cq_b = (w_cq * HEAD_DIM ** -0.5).astype(BF16)
    w_ckv_b = w_ckv.astype(BF16)
    w_co_b = w_co.astype(BF16)
    w_up_b = w_up.astype(BF16)
    w_down_b = w_down.astype(BF16)

    g_mix3 = g_mix.reshape(depth, 1, d)
    g_cross3 = g_cross.reshape(depth, 1, d)
    g_mem3 = g_mem.reshape(depth, 1, d)
    g_ffn3 = g_ffn.reshape(depth, 1, d)
    subln_col = diff_subln.reshape(depth, HEAD_DIM, 1)
    conv_b3 = conv_b.reshape(depth, 1, D_FF)
    g_final2 = g_final.reshape(1, d)

    n_ab = N_HEADS_DIFF + N_HEADS_MOBA
    bias_ab = _causal_bias_tiles(rel_bias_table[:, :n_ab] * LOG2E, ATTN_TILE)
    bias_c = jnp.stack([_dilated_bias_tiles(rel_bias_table[:, n_ab:], window, dil)
                        for window, dil in DIL_PAIRS])

    xf = x.reshape(t, d)
    memf = mem.reshape(b * n_mem, d)

    for layer in range(depth):
        qkv = _norm_matmul(xf, g_mix3, w_in_b, layer, tm=1024, tn=2048).reshape(b, s, 3 * MIX_WIDTH)
        o_a = _diff_attn(qkv, bias_ab, diff_lambda, subln_col, layer)
        o_b = _moba_attn(qkv, bias_ab)
        o_c = _dil_attn(qkv, bias_c)
        xf = _out_proj(xf, o_a.reshape(t, W_DIFF), o_b.reshape(t, W_MOBA), o_c.reshape(t, W_DIL),
                       w_out_b, layer, tm=1024, tn=1024)
        kv = _norm_matmul(memf, g_mem3, w_ckv_b, layer, tm=b * n_mem, tn=512)
        xf = _cross_attn(xf, g_cross3, w_cq_b, kv.reshape(b, n_mem, 2 * MEM_WIDTH), w_co_b, layer, s,
                         tm=512)
        xf = _conv_ffn(xf, g_ffn3, w_up_b, conv_w, conv_b3, w_down_b, g_final2, layer, s,
                       tm=512, tf=512, final_norm=(layer == depth - 1))
    return xf.reshape(b, s, d)
```
